```python
import jax, jax.numpy as jnp
from jax import lax
import numpy as np

D_MODEL = 1024
BATCH = 2
SEQ = 8192
DEPTH = 2

GRID_W = 64
CTX_LEN = 256
HG_HEAD_DIM = 128
HG_DIM = D_MODEL
HG_HEADS = HG_DIM // HG_HEAD_DIM
CHUNK = 64
CV_DIM = D_MODEL
CONV_K = 31
FFN_DIM = ((8 * D_MODEL // 3 + 255) // 256) * 256
FFN_K = 3
N_MOD = 6
EPS = 1e-6
G_MIN = 1e-6
SPLIT_SIZES = (HG_DIM, HG_DIM, HG_DIM, HG_DIM, HG_DIM, CV_DIM, CV_DIM, D_MODEL, D_MODEL)
P_TOTAL = 5 * HG_DIM + 2 * CV_DIM + 2 * D_MODEL

kernel_name = 'hybrid_hgrn2_conformer_convffn_dit'


def rmsnorm(x, w):
    xf = x.astype(jnp.float32)
    y = xf * lax.rsqrt(jnp.mean(xf * xf, axis=-1, keepdims=True) + EPS)
    return (y * w.astype(jnp.float32)).astype(x.dtype)


def layernorm(x, w, b):
    xf = x.astype(jnp.float32)
    mu = jnp.mean(xf, axis=-1, keepdims=True)
    xc = xf - mu
    y = xc * lax.rsqrt(jnp.mean(xc * xc, axis=-1, keepdims=True) + EPS)
    return (y * w.astype(jnp.float32) + b.astype(jnp.float32)).astype(x.dtype)


def split_proj(p):
    return jnp.split(p, np.cumsum(SPLIT_SIZES)[:-1], axis=-1)


def heads(t):
    return t.reshape(t.shape[0], t.shape[1], HG_HEADS, HG_HEAD_DIM)


def dwconv1d(x, w, b):
    k = w.shape[0]
    y = lax.conv_general_dilated(x, w[:, None, :], window_strides=(1,), padding=[(k // 2, k // 2)],
                                 dimension_numbers=('NWC', 'WIO', 'NWC'), feature_group_count=x.shape[-1])
    return y + b


def dwconv2d(x, w, b):
    k = w.shape[0]
    y = lax.conv_general_dilated(x, w[:, :, None, :], window_strides=(1, 1),
                                 padding=[(k // 2, k // 2), (k // 2, k // 2)],
                                 dimension_numbers=('NHWC', 'HWIO', 'NHWC'), feature_group_count=x.shape[-1])
    return y + b


def gla_chunked(q, k, v, logf, s0):
    bsz, length, nh, _ = q.shape
    dv = v.shape[-1]
    n_chunks = length // CHUNK

    def to_chunks(t):
        return t.astype(jnp.float32).reshape(bsz, n_chunks, CHUNK, nh, t.shape[-1]).transpose(1, 0, 3, 2, 4)

    tril = jnp.tril(jnp.ones((CHUNK, CHUNK), dtype=bool))[:, :, None]

    def step(state, inp):
        qc, kc, vc, fc = inp
        b = jnp.cumsum(fc, axis=2)
        o_inter = jnp.einsum('bhtd,bhde->bhte', qc * jnp.exp(b), state)
        rel = b[:, :, :, None, :] - b[:, :, None, :, :]
        decay = jnp.where(tril, jnp.exp(jnp.minimum(rel, 0.0)), 0.0)
        scores = jnp.einsum('bhtd,bhsd,bhtsd->bhts', qc, kc, decay)
        o = o_inter + jnp.einsum('bhts,bhse->bhte', scores, vc)
        b_end = b[:, :, -1]
        new_state = jnp.exp(b_end)[..., None] * state + jnp.einsum(
            'bhsd,bhse->bhde', kc * jnp.exp(jnp.minimum(b_end[:, :, None, :] - b, 0.0)), vc)
        return new_state, o

    s_fin, o = lax.scan(step, s0.astype(jnp.float32),
                        (to_chunks(q), to_chunks(k), to_chunks(v), to_chunks(logf)))
    o = o.transpose(1, 0, 3, 2, 4).reshape(bsz, length, nh, dv)
    return o, s_fin


def hgrn2_gates(f_logit, lb):
    f = f_logit.astype(jnp.float32)
    lb = lb.astype(jnp.float32)
    g = lb + (1.0 - lb) * jax.nn.sigmoid(f)
    log_g = jnp.log(jnp.clip(g, G_MIN, 1.0))
    k = (1.0 - lb) * jax.nn.sigmoid(-f)
    return k, log_g


def hgrn2_bidir(q, i, f_fwd, f_bwd, lb_fwd, lb_bwd, s_fwd, s_bwd):
    q, i = heads(q), heads(i)
    k_f, lg_f = hgrn2_gates(heads(f_fwd), lb_fwd.reshape(HG_HEADS, HG_HEAD_DIM))
    k_b, lg_b = hgrn2_gates(heads(f_bwd), lb_bwd.reshape(HG_HEADS, HG_HEAD_DIM))
    o_f, sf = gla_chunked(q, k_f, i, lg_f, s_fwd)
    flip = lambda t: jnp.flip(t, axis=1)
    o_b, sb = gla_chunked(flip(q), flip(k_b), flip(i), flip(lg_b), s_bwd)
    return o_f + flip(o_b), sf, sb


def mixer_merge(o_hg, g, cv_a, cv_b, gate_hg, gate_cv, gn_w, w_hg_out, dw_w, dw_b, ln_w, ln_b, w_cv_out, w_out):
    bsz, length = g.shape[:2]
    o = rmsnorm(o_hg.astype(g.dtype), gn_w) * jax.nn.silu(heads(g))
    y_hg = o.reshape(bsz, length, HG_DIM) @ w_hg_out
    u = cv_a * jax.nn.sigmoid(cv_b)
    u = jax.nn.silu(layernorm(dwconv1d(u, dw_w, dw_b), ln_w, ln_b))
    y_cv = u @ w_cv_out
    y = jax.nn.sigmoid(gate_hg) * y_hg + jax.nn.sigmoid(gate_cv) * y_cv
    return y @ w_out


def conv_ffn(h, w_up, dw_w, dw_b, w_down):
    u, v = jnp.split(h @ w_up, 2, axis=-1)
    u = dwconv2d(u, dw_w, dw_b)
    return (jax.nn.gelu(u, approximate=False) * v) @ w_down


def setup_inputs(seed: int = 0) -> dict:
    key = jax.random.key(seed)
    ks = jax.random.split(key, 23)
    nrm = lambda k, shape, s: s * jax.random.normal(k, shape, jnp.float32)
    return {
        'x': nrm(ks[0], (BATCH, SEQ, D_MODEL), 1.0),
        'c': nrm(ks[1], (BATCH, D_MODEL), 1.0),
        'ctx': nrm(ks[2], (BATCH, CTX_LEN, D_MODEL), 1.0),
        'c_ctx': nrm(ks[3], (D_MODEL,), 1.0),
        'w_mod': nrm(ks[4], (DEPTH, D_MODEL, N_MOD * D_MODEL), 0.5 * D_MODEL ** -0.5),
        'b_mod': nrm(ks[5], (DEPTH, N_MOD * D_MODEL), 0.01),
        'norm1_w': 1.0 + nrm(ks[6], (DEPTH, D_MODEL), 0.02),
        'w_in': nrm(ks[7], (DEPTH, D_MODEL, P_TOTAL), D_MODEL ** -0.5),
        'hg_lb_logits': nrm(ks[8], (2, DEPTH, HG_DIM), 0.5),
        'hg_gnorm_w': 1.0 + nrm(ks[9], (DEPTH, HG_HEAD_DIM), 0.02),
        'w_hg_out': nrm(ks[10], (DEPTH, HG_DIM, D_MODEL), HG_DIM ** -0.5),
        'cv_dw_w': nrm(ks[11], (DEPTH, CONV_K, CV_DIM), CONV_K ** -0.5),
        'cv_dw_b': nrm(ks[12], (DEPTH, CV_DIM), 0.01),
        'cv_ln_w': 1.0 + nrm(ks[13], (DEPTH, CV_DIM), 0.02),
        'cv_ln_b': nrm(ks[14], (DEPTH, CV_DIM), 0.01),
        'w_cv_out': nrm(ks[15], (DEPTH, CV_DIM, D_MODEL), CV_DIM ** -0.5),
        'w_out': nrm(ks[16], (DEPTH, D_MODEL, D_MODEL), D_MODEL ** -0.5),
        'norm2_w': 1.0 + nrm(ks[17], (DEPTH, D_MODEL), 0.02),
        'w_up': nrm(ks[18], (DEPTH, D_MODEL, 2 * FFN_DIM), D_MODEL ** -0.5),
        'ffn_dw_w': nrm(ks[19], (DEPTH, FFN_K, FFN_K, FFN_DIM), 1.0 / FFN_K),
        'ffn_dw_b': nrm(ks[20], (DEPTH, FFN_DIM), 0.01),
        'w_down': nrm(ks[21], (DEPTH, FFN_DIM, D_MODEL), FFN_DIM ** -0.5),
        'final_norm_w': 1.0 + nrm(ks[22], (D_MODEL,), 0.02),
    }


def reference(x, c, ctx, c_ctx, w_mod, b_mod, norm1_w, w_in, hg_lb_logits, hg_gnorm_w, w_hg_out,
              cv_dw_w, cv_dw_b, cv_ln_w, cv_ln_b, w_cv_out, w_out, norm2_w, w_up, ffn_dw_w, ffn_dw_b,
              w_down, final_norm_w):
    bsz, length, _ = x.shape
    rows = length // GRID_W
    lb_sm = jax.nn.softmax(hg_lb_logits.astype(jnp.float32), axis=1)
    lbs = jnp.cumsum(lb_sm, axis=1) - lb_sm[:, :1]
    cx = ctx
    zero_state = jnp.zeros((bsz, HG_HEADS, HG_HEAD_DIM, HG_HEAD_DIM), jnp.float32)
    for l in range(DEPTH):
        mod = (jax.nn.silu(c) @ w_mod[l] + b_mod[l])[:, None, :]
        mod_c = jax.nn.silu(c_ctx) @ w_mod[l] + b_mod[l]
        sh1, sc1, gt1, sh2, sc2, gt2 = jnp.split(mod, N_MOD, axis=-1)
        csh1, csc1, cgt1, csh2, csc2, cgt2 = jnp.split(mod_c, N_MOD, axis=-1)

        h = rmsnorm(x, norm1_w[l]) * (1 + sc1) + sh1
        hc = rmsnorm(cx, norm1_w[l]) * (1 + csc1) + csh1
        pq, pff, pfb, pi, pg, pa, pb, pgh, pgc = split_proj(h @ w_in[l])
        cq, cff, cfb, ci, cg, ca, cb, cgh, cgc = split_proj(hc @ w_in[l])
        o_ctx, s_f, s_b = hgrn2_bidir(cq, ci, cff, cfb, lbs[0, l], lbs[1, l], zero_state, zero_state)
        o_lat, _, _ = hgrn2_bidir(pq, pi, pff, pfb, lbs[0, l], lbs[1, l], s_f, s_b)
        layer_w = (hg_gnorm_w[l], w_hg_out[l], cv_dw_w[l], cv_dw_b[l], cv_ln_w[l], cv_ln_b[l], w_cv_out[l], w_out[l])
        x = x + gt1 * mixer_merge(o_lat, pg, pa, pb, pgh, pgc, *layer_w)

        h2 = (rmsnorm(x, norm2_w[l]) * (1 + sc2) + sh2).reshape(bsz, rows, GRID_W, D_MODEL)
        x = x + gt2 * conv_ffn(h2, w_up[l], ffn_dw_w[l], ffn_dw_b[l], w_down[l]).reshape(bsz, length, D_MODEL)

        if l < DEPTH - 1:
            cx = cx + cgt1 * mixer_merge(o_ctx, cg, ca, cb, cgh, cgc, *layer_w)
            hc2 = (rmsnorm(cx, norm2_w[l]) * (1 + csc2) + csh2)[:, None]
            cx = cx + cgt2 * conv_ffn(hc2, w_up[l], ffn_dw_w[l], ffn_dw_b[l], w_down[l])[:, 0]
    return rmsnorm(x, final_norm_w)
```

```python
import functools

import numpy as np
import jax
import jax.numpy as jnp
from jax import lax
from jax.experimental import pallas as pl
from jax.experimental.pallas import tpu as pltpu

F32 = jnp.float32
BF16 = jnp.bfloat16

D = 1024
H = 8
DH = 128
N_MOD = 6
GRID_W = 64
CONV_K = 31
CONV_HALO = 16
FFN = 2816
FFN_TN = 256
N_PROJ = 9
EPS = 1e-6
G_MIN = 1e-6

GLA_C = 128
GLA_LEAF = 8
LEAF_CLAMP = 60.0

VMEM_LIMIT = 56 * 1024 * 1024


def _cparams(sem):
    return pltpu.CompilerParams(dimension_semantics=sem, vmem_limit_bytes=VMEM_LIMIT)


def _mod_kernel(c_ref, w_ref, b_ref, o_ref):
    c = c_ref[...]
    a = (c * jax.nn.sigmoid(c)).astype(BF16)
    o_ref[...] = jnp.dot(a, w_ref[...].astype(BF16), preferred_element_type=F32) + b_ref[...]


def _modulation(cc, w_mod, b_mod):
    depth = w_mod.shape[0]
    tn = 1536
    return pl.pallas_call(
        _mod_kernel,
        grid=(depth, N_MOD * D // tn),
        in_specs=[pl.BlockSpec((8, D), lambda l, n: (0, 0)),
                  pl.BlockSpec((None, D, tn), lambda l, n: (l, 0, n)),
                  pl.BlockSpec((None, 1, tn), lambda l, n: (l, 0, n))],
        out_specs=pl.BlockSpec((None, 8, tn), lambda l, n: (l, 0, n)),
        out_shape=jax.ShapeDtypeStruct((depth, 8, N_MOD * D), F32),
        compiler_params=_cparams(("arbitrary", "arbitrary")),
        name="modulation",
    )(cc, w_mod, b_mod.reshape(depth, 1, N_MOD * D))


def _modulated_rmsnorm(x, nw, sc, sh):
    ms = jnp.mean(x * x, axis=-1, keepdims=True)
    return (x * lax.rsqrt(ms + EPS) * nw) * (1.0 + sc) + sh


def _inproj_kernel(x_ref, sc_ref, sh_ref, nw_ref, w_ref, o_ref, h_ref):
    @pl.when(pl.program_id(2) == 0)
    def _():
        h_ref[...] = _modulated_rmsnorm(x_ref[...], nw_ref[...], sc_ref[...], sh_ref[...]).astype(BF16)

    o_ref[...] = jnp.dot(h_ref[...], w_ref[...], preferred_element_type=F32)


def _in_projection(x, sc, sh, nw, w_bf16, tm):
    bsz, length, _ = x.shape
    p_total = w_bf16.shape[1]
    tn = 1536
    return pl.pallas_call(
        _inproj_kernel,
        grid=(bsz, length // tm, p_total // tn),
        in_specs=[pl.BlockSpec((None, tm, D), lambda b, i, n: (b, i, 0)),
                  pl.BlockSpec((None, 1, D), lambda b, i, n: (b, 0, 0)),
                  pl.BlockSpec((None, 1, D), lambda b, i, n: (b, 0, 0)),
                  pl.BlockSpec((1, D), lambda b, i, n: (0, 0)),
                  pl.BlockSpec((D, tn), lambda b, i, n: (0, n))],
        out_specs=pl.BlockSpec((None, tm, tn), lambda b, i, n: (b, i, n)),
        out_shape=jax.ShapeDtypeStruct((bsz, length, p_total), F32),
        scratch_shapes=[pltpu.VMEM((tm, D), BF16)],
        compiler_params=_cparams(("arbitrary", "arbitrary", "arbitrary")),
        name="in_projection",
    )(x, sc, sh, nw, w_bf16)


def _gla_levels():
    out = []
    c = GLA_C // 2
    while c >= GLA_LEAF:
        out.append(c)
        c //= 2
    return out


def _gla_level_ids():
    lv = _gla_levels()
    t = np.arange(GLA_C)[:, None]
    s = np.arange(GLA_C)[None, :]
    ids = np.full((GLA_C, GLA_C), -1, np.int32)
    for li, c in enumerate(lv):
        same = (t // (2 * c)) == (s // (2 * c))
        ids = np.where(same, li, ids)
    ids = np.where((t // GLA_LEAF) == (s // GLA_LEAF), len(lv), ids)
    ids = np.where(s <= t, ids, -1).astype(np.int32)
    return np.stack([ids, ids.T])


def _ref_rows(b_ref, c2, offset):
    parts = []
    for m in range(GLA_C // c2):
        p = m * c2 + offset
        parts.append(jnp.broadcast_to(b_ref[p:p + 1, :], (c2, D)))
    return parts[0] if len(parts) == 1 else jnp.concatenate(parts, axis=0)


def _gla_direction(q_ref, v_ref, f_ref, lb, tri, ids, st_ref, o_ref, k_s, b_s, qs_s, ks_s, backward):
    levels = _gla_levels()
    n_lv = len(levels)
    f = f_ref[...]
    g = lb + (1.0 - lb) * jax.nn.sigmoid(f)
    lg = jnp.log(jnp.clip(g, G_MIN, 1.0))
    k_s[...] = (1.0 - lb) * jax.nn.sigmoid(-f)
    hi = lg.astype(BF16)
    r1 = lg - hi.astype(F32)
    mid = r1.astype(BF16)
    lo = (r1 - mid.astype(F32)).astype(BF16)
    b = (jnp.dot(tri, hi, preferred_element_type=F32) + jnp.dot(tri, mid, preferred_element_type=F32)
         + jnp.dot(tri, lo, preferred_element_type=F32))
    b_s[...] = b
    end_row = 0 if backward else GLA_C - 1
    b_end = b_s[end_row:end_row + 1, :]

    q = q_ref[...]
    k = k_s[...]
    qs_s[0] = (q * jnp.exp(b)).astype(BF16)
    ks_s[0] = (k * jnp.exp(b_end - b)).astype(BF16)
    for li, c in enumerate(levels):
        r = _ref_rows(b_s, 2 * c, c if backward else c - 1)
        qs_s[1 + li] = (q * jnp.exp(jnp.minimum(b - r, 0.0))).astype(BF16)
        ks_s[1 + li] = (k * jnp.exp(jnp.minimum(r - b, 0.0))).astype(BF16)
    r = _ref_rows(b_s, GLA_LEAF, GLA_LEAF // 2 if backward else GLA_LEAF // 2 - 1)
    qs_s[1 + n_lv] = (q * jnp.exp(jnp.minimum(b - r, LEAF_CLAMP))).astype(BF16)
    ks_s[1 + n_lv] = (k * jnp.exp(jnp.minimum(r - b, LEAF_CLAMP))).astype(BF16)

    e_end = jnp.exp(b_end)
    nt = (((1,), (1,)), ((), ()))
    tn = (((0,), (0,)), ((), ()))
    for h in range(H):
        hs = slice(h * DH, (h + 1) * DH)
        p = jnp.zeros((GLA_C, GLA_C), F32)
        for li in range(n_lv + 1):
            sc = lax.dot_general(qs_s[1 + li, :, hs], ks_s[1 + li, :, hs], nt, preferred_element_type=F32)
            p = jnp.where(ids == li, sc, p)
        vh = v_ref[:, hs].astype(BF16)
        st = st_ref[h]
        o = lax.dot_general(qs_s[0, :, hs], st.astype(BF16), nt, preferred_element_type=F32)
        o = o + jnp.dot(p.astype(BF16), vh, preferred_element_type=F32)
        o_ref[:, hs] = o
        st_ref[h] = st * e_end[:, hs] + lax.dot_general(vh, ks_s[0, :, hs], tn, preferred_element_type=F32)


def _gla_kernel(qf_ref, vf_ref, ff_ref, qb_ref, vb_ref, fb_ref, lbl_ref, ids_ref, s0_ref,
                of_ref, ob_ref, sfin_ref, st_s, k_s, b_s, qs_s, ks_s, *, layer):
    j = pl.program_id(1)

    @pl.when(j == 0)
    def _():
        st_s[...] = s0_ref[...]

    row =lax.broadcasted_iota(jnp.int32, (GLA_C, GLA_C), 0)
    col = lax.broadcasted_iota(jnp.int32, (GLA_C, GLA_C), 1)
    for d, (q_ref, v_ref, f_ref, o_ref) in enumerate(((qf_ref, vf_ref, ff_ref, of_ref),
                                                      (qb_ref, vb_ref, fb_ref, ob_ref))):
        logits = lbl_ref[d]
        z = jnp.exp(logits - jnp.max(logits, axis=0, keepdims=True))
        sm = z / jnp.sum(z, axis=0, keepdims=True)
        lb = jnp.sum(sm[1:layer + 1], axis=0, keepdims=True) if layer > 0 else jnp.zeros((1, D), F32)
        backward = d == 1
        tri = jnp.where(col >= row, 1.0, 0.0) if backward else jnp.where(col <= row, 1.0, 0.0)
        _gla_direction(q_ref, v_ref, f_ref, lb, tri.astype(BF16), ids_ref[d],
                       st_s.at[d], o_ref, k_s, b_s, qs_s, ks_s, backward)

    @pl.when(j == pl.num_programs(1) - 1)
    def _():
        sfin_ref[...] = st_s[...]


def _gla(proj, lb_logits, s0, layer):
    bsz, length, _ = proj.shape
    nc = length // GLA_C
    n_slots = len(_gla_levels()) + 2
    ids = jnp.asarray(_gla_level_ids())
    fwd = lambda col: pl.BlockSpec((None, GLA_C, D), lambda b, j: (b, j, col))
    bwd = lambda col: pl.BlockSpec((None, GLA_C, D), lambda b, j: (b, nc - 1 - j, col))
    return pl.pallas_call(
        functools.partial(_gla_kernel, layer=layer),
        grid=(bsz, nc),
        in_specs=[fwd(0), fwd(3), fwd(1), bwd(0), bwd(3), bwd(2),
                  pl.BlockSpec(lb_logits.shape, lambda b, j: (0, 0, 0)),
                  pl.BlockSpec((2, GLA_C, GLA_C), lambda b, j: (0, 0, 0)),
                  pl.BlockSpec((None, 2, H, DH, DH), lambda b, j: (b, 0, 0, 0, 0))],
        out_specs=[pl.BlockSpec((None, GLA_C, D), lambda b, j: (b, j, 0)),
                   pl.BlockSpec((None, GLA_C, D), lambda b, j: (b, nc - 1 - j, 0)),
                   pl.BlockSpec((None, 2, H, DH, DH), lambda b, j: (b, 0, 0, 0, 0))],
        out_shape=[jax.ShapeDtypeStruct((bsz, length, D), F32),
                   jax.ShapeDtypeStruct((bsz, length, D), F32),
                   jax.ShapeDtypeStruct((bsz, 2, H, DH, DH), F32)],
        scratch_shapes=[pltpu.VMEM((2, H, DH, DH), F32),
                        pltpu.VMEM((GLA_C, D), F32),
                        pltpu.VMEM((GLA_C, D), F32),
                        pltpu.VMEM((n_slots, GLA_C, D), BF16),
                        pltpu.VMEM((n_slots, GLA_C, D), BF16)],
        compiler_params=_cparams(("arbitrary", "arbitrary")),
        name="gla",
    )(proj, proj, proj, proj, proj, proj, lb_logits, ids, s0)


def _mixer_kernel(*refs, tm, halo):
    if halo:
        (of_ref, ob_ref, g_ref, a_ref, b_ref, ap_ref, bp_ref, an_ref, bn_ref, gh_ref, gc_ref, x_ref, gt_ref,
         gnw_ref, dww_ref, dwb_ref, lnw_ref, lnb_ref, whg_ref, wcv_ref, wout_ref, o_ref, u_s) = refs
    else:
        (of_ref, ob_ref, g_ref, a_ref, b_ref, gh_ref, gc_ref, x_ref, gt_ref,
         gnw_ref, dww_ref, dwb_ref, lnw_ref, lnb_ref, whg_ref, wcv_ref, wout_ref, o_ref, u_s) = refs
    i = pl.program_id(1)
    n_tiles = pl.num_programs(1)

    o = of_ref[...] + ob_ref[...]
    parts = []
    for h in range(H):
        oh = o[:, h * DH:(h + 1) * DH]
        parts.append(oh * lax.rsqrt(jnp.mean(oh * oh, axis=-1, keepdims=True) + EPS))
    g = g_ref[...]
    oa = jnp.concatenate(parts, axis=-1) * gnw_ref[...] * (g * jax.nn.sigmoid(g))
    y_hg = jnp.dot(oa.astype(BF16), whg_ref[...], preferred_element_type=F32)

    glu = lambda a, b: a * jax.nn.sigmoid(b)
    u_s[CONV_HALO:CONV_HALO + tm, :] = glu(a_ref[...], b_ref[...])
    zeros = jnp.zeros((CONV_HALO, D), F32)
    if halo:
        u_s[0:CONV_HALO, :] = jnp.where(i > 0, glu(ap_ref[...], bp_ref[...]), zeros)
        u_s[CONV_HALO + tm:, :] = jnp.where(i < n_tiles - 1, glu(an_ref[...], bn_ref[...]), zeros)
    else:
        u_s[0:CONV_HALO, :] = zeros
        u_s[CONV_HALO + tm:, :] = zeros
    acc = jnp.broadcast_to(dwb_ref[...], (tm, D))
    for k in range(CONV_K):
        start = CONV_HALO - CONV_K // 2 + k
        acc = acc + dww_ref[k:k + 1, :] * u_s[start:start + tm, :]
    mu = jnp.mean(acc, axis=-1, keepdims=True)
    xc = acc - mu
    ln = xc * lax.rsqrt(jnp.mean(xc * xc, axis=-1, keepdims=True) + EPS) * lnw_ref[...] + lnb_ref[...]
    u2 = ln * jax.nn.sigmoid(ln)
    y_cv = jnp.dot(u2.astype(BF16), wcv_ref[...], preferred_element_type=F32)

    y = jax.nn.sigmoid(gh_ref[...]) * y_hg + jax.nn.sigmoid(gc_ref[...]) * y_cv
    o_ref[...] = x_ref[...] + gt_ref[...] * jnp.dot(y.astype(BF16), wout_ref[...], preferred_element_type=F32)


def _mixer(o_f, o_b, proj, x, gt, gnw, dww, dwb, lnw, lnb, whg, wcv, wout, tm):
    bsz, length, _ = x.shape
    n_tiles = length // tm
    halo = n_tiles > 1
    hb = tm // CONV_HALO
    n_hb = length // CONV_HALO
    row = lambda col: pl.BlockSpec((None, tm, D), lambda b, i: (b, i, col))
    prev = lambda col: pl.BlockSpec((None, CONV_HALO, D), lambda b, i: (b, jnp.maximum(i * hb - 1, 0), col))
    nxt = lambda col: pl.BlockSpec((None, CONV_HALO, D), lambda b, i: (b, jnp.minimum((i + 1) * hb, n_hb - 1), col))
    vec = lambda n: pl.BlockSpec((n, D), lambda b, i: (0, 0))
    mat = pl.BlockSpec((D, D), lambda b, i: (0, 0))
    in_specs = [row(0), row(0), row(4), row(5), row(6)]
    args = [o_f, o_b, proj, proj, proj]
    if halo:
        in_specs += [prev(5), prev(6), nxt(5), nxt(6)]
        args += [proj, proj, proj, proj]
    in_specs += [row(7), row(8), row(0), pl.BlockSpec((None, 1, D), lambda b, i: (b, 0, 0)),
                 vec(1), vec(CONV_K), vec(1), vec(1), vec(1), mat, mat, mat]
    args += [proj, proj, x, gt, gnw, dww, dwb, lnw, lnb, whg, wcv, wout]
    return pl.pallas_call(
        functools.partial(_mixer_kernel, tm=tm, halo=halo),
        grid=(bsz, n_tiles),
        in_specs=in_specs,
        out_specs=pl.BlockSpec((None, tm, D), lambda b, i: (b, i, 0)),
        out_shape=jax.ShapeDtypeStruct((bsz, length, D), F32),
        scratch_shapes=[pltpu.VMEM((tm + 2 * CONV_HALO, D), F32)],
        compiler_params=_cparams(("arbitrary", "arbitrary")),
        name="mixer",
    )(*args)


def _ffn_kernel(*refs, tm, gw, halo, final):
    if halo:
        (x_ref, xp_ref, xn_ref, sc_ref, sh_ref, gt_ref, nw_ref, wu_ref, wv_ref, dww_ref, dwb_ref, wd_ref, fw_ref,
         o_ref, h_s, acc_s) = refs
    else:
        (x_ref, sc_ref, sh_ref, gt_ref, nw_ref, wu_ref, wv_ref, dww_ref, dwb_ref, wd_ref, fw_ref,
         o_ref, h_s, acc_s) = refs
    i = pl.program_id(1)
    n_tiles = pl.num_programs(1)
    norm = lambda x: _modulated_rmsnorm(x, nw_ref[...], sc_ref[...], sh_ref[...])

    h_s[gw:gw + tm, :] = norm(x_ref[...]).astype(BF16)
    zeros = jnp.zeros((gw, D), BF16)
    if halo:
        h_s[0:gw, :] = jnp.where(i > 0, norm(xp_ref[...]).astype(BF16), zeros)
        h_s[gw + tm:, :] = jnp.where(i < n_tiles - 1, norm(xn_ref[...]).astype(BF16), zeros)
    else:
        h_s[0:gw, :] = zeros
        h_s[gw + tm:, :] = zeros

    ext = tm + 2 * gw
    wpos = lax.broadcasted_iota(jnp.int32, (ext, FFN_TN), 0) % gw
    first_col = wpos == 0
    last_col = wpos == gw - 1

    acc_s[...] = jnp.zeros((tm, D), F32)

    def col_tile(n, carry):
        u = jnp.dot(h_s[...], wu_ref[n], preferred_element_type=F32)
        v = jnp.dot(h_s[gw:gw + tm, :], wv_ref[n], preferred_element_type=F32)
        u_l = jnp.where(first_col, 0.0, pltpu.roll(u, 1, axis=0))
        u_r = jnp.where(last_col, 0.0, pltpu.roll(u, ext - 1, axis=0))
        dw = dww_ref[n]
        conv = jnp.broadcast_to(dwb_ref[n], (tm, FFN_TN))
        for dy in range(3):
            lo = dy * gw
            conv = conv + dw[3 * dy:3 * dy + 1] * u_l[lo:lo + tm]
            conv = conv + dw[3 * dy + 1:3 * dy + 2] * u[lo:lo + tm]
            conv = conv + dw[3 * dy + 2:3 * dy + 3] * u_r[lo:lo + tm]
        act = 0.5 * conv * (1.0 + lax.erf(conv * np.float32(1.0 / np.sqrt(2.0)))) * v
        acc_s[...] += jnp.dot(act.astype(BF16), wd_ref[n], preferred_element_type=F32)
        return carry

    lax.fori_loop(0, FFN // FFN_TN, col_tile, 0)

    y = x_ref[...] + gt_ref[...] * acc_s[...]
    if final:
        y = y * lax.rsqrt(jnp.mean(y * y, axis=-1, keepdims=True) + EPS) * fw_ref[...]
    o_ref[...] = y


def _conv_ffn(x, sc, sh, gt, nw, wu, wv, dww, dwb, wd, fw, tm, gw, final):
    bsz, length, _ = x.shape
    n_tiles = length // tm
    halo = n_tiles > 1
    rb = tm // gw
    n_rows = length // gw
    nt = FFN // FFN_TN
    const = lambda shape: pl.BlockSpec(shape, lambda b, i: (0,) * len(shape), pipeline_mode=pl.Buffered(1))
    per_b = pl.BlockSpec((None, 1, D), lambda b, i: (b, 0, 0))
    in_specs = [pl.BlockSpec((None, tm, D), lambda b, i: (b, i, 0))]
    args = [x]
    if halo:
        in_specs += [pl.BlockSpec((None, gw, D), lambda b, i: (b, jnp.maximum(i * rb - 1, 0), 0)),
                     pl.BlockSpec((None, gw, D), lambda b, i: (b, jnp.minimum((i + 1) * rb, n_rows - 1), 0))]
        args += [x, x]
    in_specs += [per_b, per_b, per_b, const((1, D)), const((nt, D, FFN_TN)), const((nt, D, FFN_TN)),
                 const((nt, 9, FFN_TN)), const((nt, 1, FFN_TN)), const((nt, FFN_TN, D)), const((1, D))]
    args += [sc, sh, gt, nw, wu, wv, dww, dwb, wd, fw]
    return pl.pallas_call(
        functools.partial(_ffn_kernel, tm=tm, gw=gw, halo=halo, final=final),
        grid=(bsz, n_tiles),
        in_specs=in_specs,
        out_specs=pl.BlockSpec((None, tm, D), lambda b, i: (b, i, 0)),
        out_shape=jax.ShapeDtypeStruct((bsz, length, D), F32),
        scratch_shapes=[pltpu.VMEM((tm + 2 * gw, D), BF16), pltpu.VMEM((tm, D), F32)],
        compiler_params=_cparams(("arbitrary", "arbitrary")),
        name="conv_ffn",
    )(*args)


def kernel(x, c, ctx, c_ctx, w_mod, b_mod, norm1_w, w_in, hg_lb_logits, hg_gnorm_w, w_hg_out, cv_dw_w, cv_dw_b,
           cv_ln_w, cv_ln_b, w_cv_out, w_out, norm2_w, w_up, ffn_dw_w, ffn_dw_b, w_down, final_norm_w):
    bsz, length, _ = x.shape
    ctx_len = ctx.shape[1]
    depth = w_mod.shape[0]
    nt = FFN // FFN_TN

    cc = jnp.zeros((8, D), F32).at[:bsz].set(c).at[bsz].set(c_ctx)
    mod = _modulation(cc, w_mod, b_mod)

    w_in_b = w_in.astype(BF16)
    whg_b, wcv_b, wout_b = w_hg_out.astype(BF16), w_cv_out.astype(BF16), w_out.astype(BF16)
    to_tiles = lambda w: w.reshape(depth, D, nt, FFN_TN).transpose(0, 2, 1, 3).astype(BF16)
    wu_b, wv_b = to_tiles(w_up[:, :, :FFN]), to_tiles(w_up[:, :, FFN:])
    wd_b = w_down.reshape(depth, nt, FFN_TN, D).astype(BF16)
    dww_t = ffn_dw_w.reshape(depth, 9, nt, FFN_TN).transpose(0, 2, 1, 3)
    dwb_t = ffn_dw_b.reshape(depth, nt, 1, FFN_TN)
    gnw = jnp.tile(hg_gnorm_w, (1, H)).reshape(depth, 1, D)
    vec = lambda a, l: a[l].reshape(1, D)
    fw = final_norm_w.reshape(1, D)

    cx = ctx
    zero_state = jnp.zeros((bsz, 2, H, DH, DH), F32)
    for l in range(depth):
        m = mod[l].reshape(8, N_MOD, D)
        lat = lambda j: m[:bsz, j].reshape(bsz, 1, D)
        cxm = lambda j: jnp.broadcast_to(m[bsz, j].reshape(1, 1, D), (bsz, 1, D))
        last = l == depth - 1

        proj_c = _in_projection(cx, cxm(1), cxm(0), vec(norm1_w, l), w_in_b[l], tm=ctx_len)
        proj = _in_projection(x, lat(1), lat(0), vec(norm1_w, l), w_in_b[l], tm=1024)
        ocf, ocb, s_ctx = _gla(proj_c, hg_lb_logits, zero_state, l)
        o_f, o_b, _ = _gla(proj, hg_lb_logits, s_ctx, l)
        mix_w = (gnw[l], cv_dw_w[l], vec(cv_dw_b, l), vec(cv_ln_w, l), vec(cv_ln_b, l), whg_b[l], wcv_b[l], wout_b[l])
        x = _mixer(o_f, o_b, proj, x, lat(2), *mix_w, tm=256)
        ffn_w = (vec(norm2_w, l), wu_b[l], wv_b[l], dww_t[l], dwb_t[l], wd_b[l], fw)
        x = _conv_ffn(x, lat(4), lat(3), lat(5), *ffn_w, tm=512, gw=GRID_W, final=last)
        if not last:
            cx = _mixer(ocf, ocb, proj_c, cx, cxm(2), *mix_w, tm=ctx_len)
            cx = _conv_ffn(cx, cxm(4), cxm(3), cxm(5), *ffn_w, tm=ctx_len, gw=ctx_len, final=False)
    return x
```

```python
import functools
import math

import numpy as np
import jax
import jax.numpy as jnp
from jax import lax
from jax.experimental import pallas as pl
from jax.experimental.pallas import tpu as pltpu

F32 = jnp.float32
BF16 = jnp.bfloat16

D = 1024
H = 8
DH = 128
N_MOD = 6
GRID_W = 64
CONV_K = 31
CONV_HALO = 16
CONV_RB = 32
SUBLANES = 8
FFN = 2816
FFN_TN = 256
N_PROJ = 9
N_PROJ_F32 = 2
PQ, PI, PG, PA, PB, PGH, PGC = range(7)
EPS = 1e-6
G_MIN = 1e-6

GLA_C = 128
GLA_CPS = 4
GLA_LEAF = 8
assert GLA_LEAF // 2 * -np.log2(G_MIN) < 120.0

VMEM_LIMIT = 56 * 1024 * 1024


def _cparams(sem):
    return pltpu.CompilerParams(dimension_semantics=sem, vmem_limit_bytes=VMEM_LIMIT)


def _mod_kernel(c_ref, w_ref, b_ref, o_ref):
    c = c_ref[...]
    a = (c * jax.nn.sigmoid(c)).astype(BF16)
    o_ref[...] = jnp.dot(a, w_ref[...].astype(BF16), preferred_element_type=F32) + b_ref[...]


def _modulation(cc, w_mod, b_mod):
    depth = w_mod.shape[0]
    tn = 1536
    return pl.pallas_call(
        _mod_kernel,
        grid=(depth, N_MOD * D // tn),
        in_specs=[pl.BlockSpec((8, D), lambda l, n: (0, 0)),
                  pl.BlockSpec((None, D, tn), lambda l, n: (l, 0, n)),
                  pl.BlockSpec((None, 1, tn), lambda l, n: (l, 0, n))],
        out_specs=pl.BlockSpec((None, 8, tn), lambda l, n: (l, 0, n)),
        out_shape=jax.ShapeDtypeStruct((depth, 8, N_MOD * D), F32),
        compiler_params=_cparams(("arbitrary", "arbitrary")),
        name="modulation",
    )(cc, w_mod, b_mod.reshape(depth, 1, N_MOD * D))


def _modulated_rmsnorm(x, nw, sc, sh):
    ms = jnp.mean(x * x, axis=-1, keepdims=True)
    return (x * lax.rsqrt(ms + EPS) * nw) * (1.0 + sc) + sh


def _inproj_kernel(x_ref, sc_ref, sh_ref, nw_ref, w_ref, of_ref, ob_ref):
    h = _modulated_rmsnorm(x_ref[...], nw_ref[...], sc_ref[...], sh_ref[...]).astype(BF16)
    for n in range(N_PROJ):
        y = jnp.dot(h, w_ref[n], preferred_element_type=F32)
        if n < N_PROJ_F32:
            of_ref[:, n * D:(n + 1) * D] = y
        else:
            m = n - N_PROJ_F32
            ob_ref[:, m * D:(m + 1) * D] = y.astype(BF16)


def _in_projection(x, sc, sh, nw, w_groups, tm):
    bsz, length, _ = x.shape
    n_b16 = N_PROJ - N_PROJ_F32
    return pl.pallas_call(
        _inproj_kernel,
        grid=(bsz, length // tm),
        in_specs=[pl.BlockSpec((None, tm, D), lambda b, i: (b, i, 0)),
                  pl.BlockSpec((None, 1, D), lambda b, i: (b, 0, 0)),
                  pl.BlockSpec((None, 1, D), lambda b, i: (b, 0, 0)),
                  pl.BlockSpec((1, D), lambda b, i: (0, 0)),
                  pl.BlockSpec((N_PROJ, D, D), lambda b, i: (0, 0, 0), pipeline_mode=pl.Buffered(1))],
        out_specs=[pl.BlockSpec((None, tm, N_PROJ_F32 * D), lambda b, i: (b, i, 0)),
                   pl.BlockSpec((None, tm, n_b16 * D), lambda b, i: (b, i, 0))],
        out_shape=[jax.ShapeDtypeStruct((bsz, length, N_PROJ_F32 * D), F32),
                   jax.ShapeDtypeStruct((bsz, length, n_b16 * D), BF16)],
        compiler_params=_cparams(("arbitrary", "arbitrary")),
        name="in_projection",
    )(x, sc, sh, nw, w_groups)


def _gla_levels():
    out = []
    c = GLA_C // 2
    while c >= GLA_LEAF:
        out.append(c)
        c //= 2
    return out


def _gla_level_ids():
    lv = _gla_levels()
    t = np.arange(GLA_C)[:, None]
    s = np.arange(GLA_C)[None, :]
    ids = np.full((GLA_C, GLA_C), -1, np.int32)
    for li, c in enumerate(lv):
        same = (t // (2 * c)) == (s // (2 * c))
        ids = np.where(same, li, ids)
    ids = np.where((t // GLA_LEAF) == (s // GLA_LEAF), len(lv), ids)
    ids = np.where(s <= t, ids, -1).astype(np.int32)
    return np.stack([ids, ids.T])


def _ref_rows(b_ref, c2, offset):
    parts = []
    for m in range(GLA_C // c2):
        p = m * c2 + offset
        parts.append(jnp.broadcast_to(b_ref[p:p + 1, :], (c2, DH)))
    return parts[0] if len(parts) == 1 else jnp.concatenate(parts, axis=0)


def _gla_head(q_ref, v_ref, f_ref, o_ref, lb, tri, ids, st_ref, b_ref, backward):
    levels = _gla_levels()
    n_lv = len(levels)
    f = f_ref[...]
    z = jnp.exp(-jnp.abs(f))
    s_big = 1.0 / (1.0 + z)
    s_small = z * s_big
    pos = f >= 0.0
    g = lb + (1.0 - lb) * jnp.where(pos, s_big, s_small)
    lg = jnp.log2(jnp.clip(g, G_MIN, 1.0))
    k = (1.0 - lb) * jnp.where(pos, s_small, s_big)
    hi = lg.astype(BF16)
    r1 = lg - hi.astype(F32)
    mid = r1.astype(BF16)
    lo = (r1 - mid.astype(F32)).astype(BF16)
    b = (jnp.dot(tri, hi, preferred_element_type=F32) + jnp.dot(tri, mid, preferred_element_type=F32)
         + jnp.dot(tri, lo, preferred_element_type=F32))
    yield
    b_ref[...] = b
    end_row = 0 if backward else GLA_C - 1
    b_end = b_ref[end_row:end_row + 1, :]
    q = q_ref[...].astype(F32)

    nt = (((1,), (1,)), ((), ()))
    tn = (((0,), (0,)), ((), ()))
    sc = []
    for li, c in enumerate(levels):
        parts = []
        for m in range(GLA_C // (2 * c)):
            p = 2 * c * m + (c if backward else c - 1)
            r = jnp.broadcast_to(b_ref[p:p + 1, :], (c, DH))
            first, second = b[2 * c * m:2 * c * m + c], b[2 * c * m + c:2 * c * (m + 1)]
            parts += [first - r, r - second] if backward else [r - first, second - r]
        e = jnp.exp2(jnp.concatenate(parts, axis=0))
        sc.append(lax.dot_general((q * e).astype(BF16), (k * e).astype(BF16), nt, preferred_element_type=F32))
        yield
    r = _ref_rows(b_ref, GLA_LEAF, GLA_LEAF // 2 if backward else GLA_LEAF // 2 - 1)
    x = b - r
    sc.append(lax.dot_general((q * jnp.exp2(x)).astype(BF16), (k * jnp.exp2(-x)).astype(BF16), nt,
                              preferred_element_type=F32))
    st = st_ref[...]
    o_inter = lax.dot_general((q * jnp.exp2(b)).astype(BF16), st.astype(BF16), nt, preferred_element_type=F32)
    yield
    rows = []
    for a in range(GLA_C // GLA_LEAF):
        rs = slice(a * GLA_LEAF, (a + 1) * GLA_LEAF)
        ida = ids[rs]
        pa = jnp.where(ida == n_lv, sc[n_lv][rs], 0.0)
        for li, c in enumerate(levels):
            in_query_half = ((a * GLA_LEAF) // c) % 2 == (0 if backward else 1)
            if in_query_half:
                pa = jnp.where(ida == li, sc[li][rs], pa)
        rows.append(pa)
    p = jnp.concatenate(rows, axis=0)

    vb = v_ref[...]
    o_intra = jnp.dot(p.astype(BF16), vb, preferred_element_type=F32)
    k_end = (k * jnp.exp2(b_end - b)).astype(BF16)
    st_new = lax.dot_general(vb, k_end, tn, preferred_element_type=F32)
    yield
    o_ref[...] = o_inter + o_intra
    st_ref[...] = st * jnp.exp2(b_end) + st_new


def _interleave(units, skew, dep):
    pending = list(enumerate(units))
    active = []
    done = set()
    tick = 0
    while pending or active:
        if pending and tick % skew == 0:
            n = pending[0][0]
            assert n < dep or n - dep in done
            active.append(pending.pop(0))
        for n, g in list(active):
            try:
                next(g)
            except StopIteration:
                active.remove((n, g))
                done.add(n)
        tick += 1


def _gla_kernel(qf_ref, vf_ref, ff_ref, qb_ref, vb_ref, fb_ref, lbl_ref, ids_ref, s0_ref,
                of_ref, ob_ref, sfin_ref, st_s, b_s, *, layer, cps):
    j = pl.program_id(1)

    @pl.when(j == 0)
    def _():
        st_s[...] = s0_ref[...]

    row =lax.broadcasted_iota(jnp.int32, (GLA_C, GLA_C), 0)
    col = lax.broadcasted_iota(jnp.int32, (GLA_C, GLA_C), 1)
    units = [[[], []] for _ in range(cps)]
    for d, (q_ref, v_ref, f_ref, o_ref) in enumerate(((qf_ref, vf_ref, ff_ref, of_ref),
                                                      (qb_ref, vb_ref, fb_ref, ob_ref))):
        logits = lbl_ref[d]
        z = jnp.exp(logits - jnp.max(logits, axis=0, keepdims=True))
        sm = z / jnp.sum(z, axis=0, keepdims=True)
        lb = jnp.sum(sm[1:layer + 1], axis=0, keepdims=True) if layer > 0 else jnp.zeros((1, D), F32)
        backward = d == 1
        tri = jnp.where(col >= row, 1.0, 0.0) if backward else jnp.where(col <= row, 1.0, 0.0)
        tri = tri.astype(BF16)
        ids = ids_ref[d]
        for pos in range(cps):
            c = cps - 1 - pos if backward else pos
            rs = slice(c * GLA_C, (c + 1) * GLA_C)
            for h in range(H):
                hs = slice(h * DH, (h + 1) * DH)
                units[pos][d].append(_gla_head(q_ref.at[rs, hs], v_ref.at[rs, hs], f_ref.at[rs, hs],
                                               o_ref.at[rs, hs], lb[:, hs], tri, ids, st_s.at[d, h],
                                               b_s.at[pos, d, h], backward))
    _interleave([u for per_dir in units for pair in zip(*per_dir) for u in pair], skew=1, dep=2 * H)

    @pl.when(j == pl.num_programs(1) - 1)
    def _():
        sfin_ref[...] = st_s[...]


def _gla(proj_f, proj_b, lb_logits, s0, layer):
    bsz, length, _ = proj_f.shape
    cps = math.gcd(GLA_CPS, length // GLA_C)
    rows = cps * GLA_C
    nc = length // rows
    ids = jnp.asarray(_gla_level_ids())
    fwd = lambda col: pl.BlockSpec((None, rows, D), lambda b, j: (b, j, col))
    bwd = lambda col: pl.BlockSpec((None, rows, D), lambda b, j: (b, nc - 1 - j, col))
    return pl.pallas_call(
        functools.partial(_gla_kernel, layer=layer, cps=cps),
        grid=(bsz, nc),
        in_specs=[fwd(PQ), fwd(PI), fwd(0), bwd(PQ), bwd(PI), bwd(1),
                  pl.BlockSpec(lb_logits.shape, lambda b, j: (0, 0, 0)),
                  pl.BlockSpec((2, GLA_C, GLA_C), lambda b, j: (0, 0, 0)),
                  pl.BlockSpec((None, 2, H, DH, DH), lambda b, j: (b, 0, 0, 0, 0))],
        out_specs=[pl.BlockSpec((None, rows, D), lambda b, j: (b, j, 0)),
                   pl.BlockSpec((None, rows, D), lambda b, j: (b, nc - 1 - j, 0)),
                   pl.BlockSpec((None, 2, H, DH, DH), lambda b, j: (b, 0, 0, 0, 0))],
        out_shape=[jax.ShapeDtypeStruct((bsz, length, D), F32),
                   jax.ShapeDtypeStruct((bsz, length, D), F32),
                   jax.ShapeDtypeStruct((bsz, 2, H, DH, DH), F32)],
        scratch_shapes=[pltpu.VMEM((2, H, DH, DH), F32),
                        pltpu.VMEM((cps, 2, H, GLA_C, DH), F32)],
        compiler_params=_cparams(("arbitrary", "arbitrary")),
        name="gla",
    )(proj_b, proj_b, proj_f, proj_b, proj_b, proj_f, lb_logits, ids, s0)


def _mixer_kernel(*refs, tm, halo):
    if halo:
        (of_ref, ob_ref, g_ref, a_ref, b_ref, ap_ref, bp_ref, an_ref, bn_ref, gh_ref, gc_ref, x_ref, gt_ref,
         gnw_ref, dww_ref, dwb_ref, lnw_ref, lnb_ref, whg_ref, wcv_ref, wout_ref, o_ref, u_s, sh_s, u2_s) = refs
    else:
        (of_ref, ob_ref, g_ref, a_ref, b_ref, gh_ref, gc_ref, x_ref, gt_ref,
         gnw_ref, dww_ref, dwb_ref, lnw_ref, lnb_ref, whg_ref, wcv_ref, wout_ref, o_ref, u_s, sh_s, u2_s) = refs
    i = pl.program_id(1)
    n_tiles = pl.num_programs(1)

    o = of_ref[...] + ob_ref[...]
    parts = []
    for h in range(H):
        oh = o[:, h * DH:(h + 1) * DH]
        parts.append(oh * lax.rsqrt(jnp.mean(oh * oh, axis=-1, keepdims=True) + EPS))
    g = g_ref[...].astype(F32)
    oa =jnp.concatenate(parts, axis=-1) * gnw_ref[...] * (g * jax.nn.sigmoid(g))
    y_hg = jnp.dot(oa.astype(BF16), whg_ref[...], preferred_element_type=F32)

    glu = lambda a, b: a[...].astype(F32) * jax.nn.sigmoid(b[...].astype(F32))
    u_s[CONV_HALO:CONV_HALO + tm, :] = glu(a_ref, b_ref)
    zeros = jnp.zeros((CONV_HALO, D), F32)
    if halo:
        u_s[0:CONV_HALO, :] = jnp.where(i > 0, glu(ap_ref, bp_ref), zeros)
        u_s[CONV_HALO + tm:, :] = jnp.where(i < n_tiles - 1, glu(an_ref, bn_ref), zeros)
    else:
        u_s[0:CONV_HALO, :] = zeros
        u_s[CONV_HALO + tm:, :] = zeros
    ext = tm + 2 * CONV_HALO
    n_sh = ext - SUBLANES
    ub = u_s[...]
    for s in range(SUBLANES):
        rolled = ub if s == 0 else pltpu.roll(ub, ext - s, axis=0)
        sh_s[s] = rolled[:n_sh].reshape(n_sh // SUBLANES, SUBLANES, D)

    rb = CONV_RB // SUBLANES

    def conv_block(blk, carry):
        acc = jnp.broadcast_to(dwb_ref[...], (rb, SUBLANES, D))
        first = CONV_HALO - CONV_K // 2
        for s in range(SUBLANES):
            taps = [k for k in range(CONV_K) if (first + k) % SUBLANES == s]
            tiles = [(first + k) // SUBLANES for k in taps]
            win = sh_s[s, pl.ds(blk * rb + tiles[0], rb + tiles[-1] - tiles[0])]
            for k, t in zip(taps, tiles):
                acc = acc + dww_ref[k] * win[t - tiles[0]:t - tiles[0] + rb]
        mu = jnp.mean(acc, axis=-1, keepdims=True)
        xc = acc - mu
        ln = xc * lax.rsqrt(jnp.mean(xc * xc, axis=-1, keepdims=True) + EPS) * lnw_ref[...] + lnb_ref[...]
        u2 = (ln * jax.nn.sigmoid(ln)).reshape(CONV_RB, D)
        u2_s[pl.ds(pl.multiple_of(blk * CONV_RB, CONV_RB), CONV_RB), :] = u2.astype(BF16)
        return carry

    lax.fori_loop(0, tm // CONV_RB, conv_block, 0, unroll=True)
    y_cv = jnp.dot(u2_s[...], wcv_ref[...], preferred_element_type=F32)

    y = jax.nn.sigmoid(gh_ref[...].astype(F32)) * y_hg + jax.nn.sigmoid(gc_ref[...].astype(F32)) * y_cv
    o_ref[...] = x_ref[...] + gt_ref[...] * jnp.dot(y.astype(BF16), wout_ref[...], preferred_element_type=F32)


def _mixer(o_f, o_b, proj, x, gt, gnw, dww, dwb, lnw, lnb, whg, wcv, wout, tm):
    bsz, length, _ = x.shape
    n_tiles = length // tm
    halo = n_tiles > 1
    hb = tm // CONV_HALO
    n_hb = length // CONV_HALO
    row = lambda col: pl.BlockSpec((None, tm, D), lambda b, i: (b, i, col))
    prev = lambda col: pl.BlockSpec((None, CONV_HALO, D), lambda b, i: (b, jnp.maximum(i * hb - 1, 0), col))
    nxt = lambda col: pl.BlockSpec((None, CONV_HALO, D), lambda b, i: (b, jnp.minimum((i + 1) * hb, n_hb - 1), col))
    vec = lambda n: pl.BlockSpec((n, D), lambda b, i: (0, 0))
    mat = pl.BlockSpec((D, D), lambda b, i: (0, 0))
    in_specs = [row(0), row(0), row(PG), row(PA), row(PB)]
    args = [o_f, o_b, proj, proj, proj]
    if halo:
        in_specs += [prev(PA), prev(PB), nxt(PA), nxt(PB)]
        args += [proj, proj, proj, proj]
    in_specs += [row(PGH), row(PGC), row(0), pl.BlockSpec((None, 1, D), lambda b, i: (b, 0, 0)),
                 vec(1), pl.BlockSpec((CONV_K, SUBLANES, D), lambda b, i: (0, 0, 0)), vec(1), vec(1), vec(1),
                 mat, mat, mat]
    dww8 = jnp.broadcast_to(dww[:, None, :], (CONV_K, SUBLANES, D))
    args += [proj, proj, x, gt, gnw, dww8, dwb, lnw, lnb, whg, wcv, wout]
    ext = tm + 2 * CONV_HALO
    return pl.pallas_call(
        functools.partial(_mixer_kernel, tm=tm, halo=halo),
        grid=(bsz, n_tiles),
        in_specs=in_specs,
        out_specs=pl.BlockSpec((None, tm, D), lambda b, i: (b, i, 0)),
        out_shape=jax.ShapeDtypeStruct((bsz, length, D), F32),
        scratch_shapes=[pltpu.VMEM((ext, D), F32),
                        pltpu.VMEM((SUBLANES, (ext - SUBLANES) // SUBLANES, SUBLANES, D), F32),
                        pltpu.VMEM((tm, D), BF16)],
        compiler_params=_cparams(("arbitrary", "arbitrary")),
        name="mixer",
    )(*args)


def _ffn_kernel(*refs, tm, gw, halo, final):
    if halo:
        (x_ref, xp_ref, xn_ref, sc_ref, sh_ref, gt_ref, nw_ref, wu_ref, wv_ref, dww_ref, dwb_ref, wd_ref, fw_ref,
         o_ref, h_s, act_s) = refs
    else:
        (x_ref, sc_ref, sh_ref, gt_ref, nw_ref, wu_ref, wv_ref, dww_ref, dwb_ref, wd_ref, fw_ref,
         o_ref, h_s, act_s) = refs
    i = pl.program_id(1)
    n_tiles = pl.num_programs(1)
    norm = lambda x: _modulated_rmsnorm(x, nw_ref[...], sc_ref[...], sh_ref[...])

    h_s[gw:gw + tm, :] = norm(x_ref[...]).astype(BF16)
    zeros = jnp.zeros((gw, D), BF16)
    if halo:
        h_s[0:gw, :] = jnp.where(i > 0, norm(xp_ref[...]).astype(BF16), zeros)
        h_s[gw + tm:, :] = jnp.where(i < n_tiles - 1, norm(xn_ref[...]).astype(BF16), zeros)
    else:
        h_s[0:gw, :] = zeros
        h_s[gw + tm:, :] = zeros

    ext = tm + 2 * gw
    wpos = lax.broadcasted_iota(jnp.int32, (ext, FFN_TN), 0) % gw
    first_col = wpos == 0
    last_col = wpos == gw - 1

    def col_tile(n, carry):
        u = jnp.dot(h_s[...], wu_ref[n], preferred_element_type=F32)
        v = jnp.dot(h_s[gw:gw + tm, :], wv_ref[n], preferred_element_type=F32)
        u_l = jnp.where(first_col, 0.0, pltpu.roll(u, 1, axis=0))
        u_r = jnp.where(last_col, 0.0, pltpu.roll(u, ext - 1, axis=0))
        dw = dww_ref[n]
        conv = jnp.broadcast_to(dwb_ref[n], (tm, FFN_TN))
        for dy in range(3):
            lo = dy * gw
            conv = conv + dw[3 * dy:3 * dy + 1] * u_l[lo:lo + tm]
            conv = conv + dw[3 * dy + 1:3 * dy + 2] * u[lo:lo + tm]
            conv = conv + dw[3 * dy + 2:3 * dy + 3] * u_r[lo:lo + tm]
        act = 0.5 * conv * (1.0 + lax.erf(conv * np.float32(1.0 / np.sqrt(2.0)))) * v
        act_s[n] = act.astype(BF16)
        return carry

    nt = FFN // FFN_TN
    lax.fori_loop(0, nt, col_tile, 0, unroll=True)

    down = jnp.dot(act_s[0], wd_ref[0], preferred_element_type=F32)
    for n in range(1, nt):
        down = down + jnp.dot(act_s[n], wd_ref[n], preferred_element_type=F32)
    y = x_ref[...] + gt_ref[...] * down
    if final:
        y = y * lax.rsqrt(jnp.mean(y * y, axis=-1, keepdims=True) + EPS) * fw_ref[...]
    o_ref[...] = y


def _conv_ffn(x, sc, sh, gt, nw, wu, wv, dww, dwb, wd, fw, tm, gw, final):
    bsz, length, _ = x.shape
    n_tiles = length // tm
    halo = n_tiles > 1
    rb = tm // gw
    n_rows = length // gw
    nt = FFN // FFN_TN
    const = lambda shape: pl.BlockSpec(shape, lambda b, i: (0,) * len(shape), pipeline_mode=pl.Buffered(1))
    per_b = pl.BlockSpec((None, 1, D), lambda b, i: (b, 0, 0))
    in_specs = [pl.BlockSpec((None, tm, D), lambda b, i: (b, i, 0))]
    args = [x]
    if halo:
        in_specs += [pl.BlockSpec((None, gw, D), lambda b, i: (b, jnp.maximum(i * rb - 1, 0), 0)),
                     pl.BlockSpec((None, gw, D), lambda b, i: (b, jnp.minimum((i + 1) * rb, n_rows - 1), 0))]
        args += [x, x]
    in_specs += [per_b, per_b, per_b, const((1, D)), const((nt, D, FFN_TN)), const((nt, D, FFN_TN)),
                 const((nt, 9, FFN_TN)), const((nt, 1, FFN_TN)), const((nt, FFN_TN, D)), const((1, D))]
    args += [sc, sh, gt, nw, wu, wv, dww, dwb, wd, fw]
    return pl.pallas_call(
        functools.partial(_ffn_kernel, tm=tm, gw=gw, halo=halo, final=final),
        grid=(bsz, n_tiles),
        in_specs=in_specs,
        out_specs=pl.BlockSpec((None, tm, D), lambda b, i: (b, i, 0)),
        out_shape=jax.ShapeDtypeStruct((bsz, length, D), F32),
        scratch_shapes=[pltpu.VMEM((tm + 2 * gw, D), BF16), pltpu.VMEM((nt, tm, FFN_TN), BF16)],
        compiler_params=_cparams(("arbitrary", "arbitrary")),
        name="conv_ffn",
    )(*args)


def kernel(x, c, ctx, c_ctx, w_mod, b_mod, norm1_w, w_in, hg_lb_logits, hg_gnorm_w, w_hg_out, cv_dw_w, cv_dw_b,
           cv_ln_w, cv_ln_b, w_cv_out, w_out, norm2_w, w_up, ffn_dw_w, ffn_dw_b, w_down, final_norm_w):
    bsz, length, _ = x.shape
    ctx_len = ctx.shape[1]
    depth = w_mod.shape[0]
    nt = FFN // FFN_TN

    cc = jnp.zeros((8, D), F32).at[:bsz].set(c).at[bsz].set(c_ctx)
    mod = _modulation(cc, w_mod, b_mod)

    order = np.array([1, 2, 0, 3, 4, 5, 6, 7, 8])
    w_in_g = w_in.reshape(depth, D, N_PROJ, D).transpose(0, 2, 1, 3)[:, order].astype(BF16)
    whg_b, wcv_b, wout_b = w_hg_out.astype(BF16), w_cv_out.astype(BF16), w_out.astype(BF16)
    to_tiles = lambda w: w.reshape(depth, D, nt, FFN_TN).transpose(0, 2, 1, 3).astype(BF16)
    wu_b, wv_b = to_tiles(w_up[:, :, :FFN]), to_tiles(w_up[:, :, FFN:])
    wd_b = w_down.reshape(depth, nt, FFN_TN, D).astype(BF16)
    dww_t = ffn_dw_w.reshape(depth, 9, nt, FFN_TN).transpose(0, 2, 1, 3)
    dwb_t = ffn_dw_b.reshape(depth, nt, 1, FFN_TN)
    gnw = jnp.tile(hg_gnorm_w, (1, H)).reshape(depth, 1, D)
    vec = lambda a, l: a[l].reshape(1, D)
    fw = final_norm_w.reshape(1, D)

    cx = ctx
    zero_state = jnp.zeros((bsz, 2, H, DH, DH), F32)
    for l in range(depth):
        m = mod[l].reshape(8, N_MOD, D)
        lat = lambda j: m[:bsz, j].reshape(bsz, 1, D)
        cxm = lambda j: jnp.broadcast_to(m[bsz, j].reshape(1, 1, D), (bsz, 1, D))
        last = l == depth - 1

        pcf, pcb = _in_projection(cx, cxm(1), cxm(0), vec(norm1_w, l), w_in_g[l], tm=ctx_len)
        pf, pb = _in_projection(x, lat(1), lat(0), vec(norm1_w, l), w_in_g[l], tm=512)
        ocf, ocb, s_ctx = _gla(pcf, pcb, hg_lb_logits, zero_state, l)
        o_f, o_b, _ = _gla(pf, pb, hg_lb_logits, s_ctx, l)
        mix_w = (gnw[l], cv_dw_w[l], vec(cv_dw_b, l), vec(cv_ln_w, l), vec(cv_ln_b, l), whg_b[l], wcv_b[l], wout_b[l])
        x = _mixer(o_f, o_b, pb, x, lat(2), *mix_w, tm=256)
        ffn_w = (vec(norm2_w, l), wu_b[l], wv_b[l], dww_t[l], dwb_t[l], wd_b[l], fw)
        x = _conv_ffn(x, lat(4), lat(3), lat(5), *ffn_w, tm=512, gw=GRID_W, final=last)
        if not last:
            cx = _mixer(ocf, ocb, pcb, cx, cxm(2), *mix_w, tm=ctx_len)
            cx = _conv_ffn(cx, cxm(4), cxm(3), cxm(5), *ffn_w, tm=ctx_len, gw=ctx_len, final=False)
    return x
```

```python
import functools
import math

import numpy as np
import jax
import jax.numpy as jnp
from jax import lax
from jax.experimental import pallas as pl
from jax.experimental.pallas import tpu as pltpu

F32 = jnp.float32
BF16 = jnp.bfloat16

D = 1024
H = 8
DH = 128
N_MOD = 6
GRID_W = 64
CONV_K = 31
CONV_HALO = 16
CONV_RB = 32
SUBLANES = 8
FFN = 2816
FFN_TN = 256
N_PROJ = 9
N_PROJ_F32 = 2
PROJ_SRC = (1, 2, 0, 3, 4, 5, 6, 7, 8)
PQ, PI, PG, PA, PB, PGH, PGC = range(7)
EPS = 1e-6
G_MIN = 1e-6

GLA_C = 128
GLA_CPS = 4
GLA_LEAF = 8
assert GLA_LEAF // 2 * -np.log2(G_MIN) < 120.0

VMEM_LIMIT = 56 * 1024 * 1024


def _cparams(sem):
    return pltpu.CompilerParams(dimension_semantics=sem, vmem_limit_bytes=VMEM_LIMIT)


def _mod_kernel(c_ref, w_ref, b_ref, o_ref):
    c = c_ref[...]
    a = (c * jax.nn.sigmoid(c)).astype(BF16)
    o_ref[...] = jnp.dot(a, w_ref[...].astype(BF16), preferred_element_type=F32) + b_ref[...]


def _modulation(cc, w_mod, b_mod):
    depth = w_mod.shape[0]
    tn = 1536
    return pl.pallas_call(
        _mod_kernel,
        grid=(depth, N_MOD * D // tn),
        in_specs=[pl.BlockSpec((8, D), lambda l, n: (0, 0)),
                  pl.BlockSpec((None, D, tn), lambda l, n: (l, 0, n)),
                  pl.BlockSpec((None, 1, tn), lambda l, n: (l, 0, n))],
        out_specs=pl.BlockSpec((None, 8, tn), lambda l, n: (l, 0, n)),
        out_shape=jax.ShapeDtypeStruct((depth, 8, N_MOD * D), F32),
        compiler_params=_cparams(("arbitrary", "arbitrary")),
        name="modulation",
    )(cc, w_mod, b_mod.reshape(depth, 1, N_MOD * D))


def _modulated_rmsnorm(x, nw, sc, sh):
    ms = jnp.mean(x * x, axis=-1, keepdims=True)
    return (x * lax.rsqrt(ms + EPS) * nw) * (1.0 + sc) + sh


def _inproj_kernel(x_ref, sc_ref, sh_ref, nw_ref, w_ref, of_ref, ob_ref):
    h = _modulated_rmsnorm(x_ref[...], nw_ref[...], sc_ref[...], sh_ref[...]).astype(BF16)
    for n, src in enumerate(PROJ_SRC):
        y = jnp.dot(h, w_ref[:, src * D:(src + 1) * D], preferred_element_type=F32)
        if n < N_PROJ_F32:
            of_ref[:, n * D:(n + 1) * D] = y
        else:
            m = n - N_PROJ_F32
            ob_ref[:, m * D:(m + 1) * D] = y.astype(BF16)


def _in_projection(x, sc, sh, nw, w_bf16, tm):
    bsz, length, _ = x.shape
    n_b16 = N_PROJ - N_PROJ_F32
    return pl.pallas_call(
        _inproj_kernel,
        grid=(bsz, length // tm),
        in_specs=[pl.BlockSpec((None, tm, D), lambda b, i: (b, i, 0)),
                  pl.BlockSpec((None, 1, D), lambda b, i: (b, 0, 0)),
                  pl.BlockSpec((None, 1, D), lambda b, i: (b, 0, 0)),
                  pl.BlockSpec((1, D), lambda b, i: (0, 0)),
                  pl.BlockSpec((D, N_PROJ * D), lambda b, i: (0, 0), pipeline_mode=pl.Buffered(1))],
        out_specs=[pl.BlockSpec((None, tm, N_PROJ_F32 * D), lambda b, i: (b, i, 0)),
                   pl.BlockSpec((None, tm, n_b16 * D), lambda b, i: (b, i, 0))],
        out_shape=[jax.ShapeDtypeStruct((bsz, length, N_PROJ_F32 * D), F32),
                   jax.ShapeDtypeStruct((bsz, length, n_b16 * D), BF16)],
        compiler_params=_cparams(("arbitrary", "arbitrary")),
        name="in_projection",
    )(x, sc, sh, nw, w_bf16)


def _gla_levels():
    out = []
    c = GLA_C // 2
    while c >= GLA_LEAF:
        out.append(c)
        c //= 2
    return out


def _gla_level_ids():
    lv = _gla_levels()
    t = np.arange(GLA_C)[:, None]
    s = np.arange(GLA_C)[None, :]
    ids = np.full((GLA_C, GLA_C), -1, np.int32)
    for li, c in enumerate(lv):
        same = (t // (2 * c)) == (s // (2 * c))
        ids = np.where(same, li, ids)
    ids = np.where((t // GLA_LEAF) == (s // GLA_LEAF), len(lv), ids)
    ids = np.where(s <= t, ids, -1).astype(np.int32)
    return np.stack([ids, ids.T])


def _ref_rows(b_ref, c2, offset):
    parts = []
    for m in range(GLA_C // c2):
        p = m * c2 + offset
        parts.append(jnp.broadcast_to(b_ref[p:p + 1, :], (c2, DH)))
    return parts[0] if len(parts) == 1 else jnp.concatenate(parts, axis=0)


def _gla_head(q_ref, v_ref, f_ref, o_ref, lb, tri, ids, st_ref, b_ref, backward):
    levels = _gla_levels()
    n_lv = len(levels)
    f = f_ref[...]
    z = jnp.exp(-jnp.abs(f))
    s_big = 1.0 / (1.0 + z)
    s_small = z * s_big
    pos = f >= 0.0
    g = lb + (1.0 - lb) * jnp.where(pos, s_big, s_small)
    lg = jnp.log2(jnp.clip(g, G_MIN, 1.0))
    k = (1.0 - lb) * jnp.where(pos, s_small, s_big)
    hi = lg.astype(BF16)
    r1 = lg - hi.astype(F32)
    mid = r1.astype(BF16)
    lo = (r1 - mid.astype(F32)).astype(BF16)
    b = (jnp.dot(tri, hi, preferred_element_type=F32) + jnp.dot(tri, mid, preferred_element_type=F32)
         + jnp.dot(tri, lo, preferred_element_type=F32))
    yield
    b_ref[...] = b
    end_row = 0 if backward else GLA_C - 1
    b_end = b_ref[end_row:end_row + 1, :]
    q = q_ref[...].astype(F32)

    nt = (((1,), (1,)), ((), ()))
    tn = (((0,), (0,)), ((), ()))
    sc = []
    for li, c in enumerate(levels):
        parts = []
        for m in range(GLA_C // (2 * c)):
            p = 2 * c * m + (c if backward else c - 1)
            r = jnp.broadcast_to(b_ref[p:p + 1, :], (c, DH))
            first, second = b[2 * c * m:2 * c * m + c], b[2 * c * m + c:2 * c * (m + 1)]
            parts += [first - r, r - second] if backward else [r - first, second - r]
        e = jnp.exp2(jnp.concatenate(parts, axis=0))
        sc.append(lax.dot_general((q * e).astype(BF16), (k * e).astype(BF16), nt, preferred_element_type=F32))
        yield
    r = _ref_rows(b_ref, GLA_LEAF, GLA_LEAF // 2 if backward else GLA_LEAF // 2 - 1)
    x = b - r
    sc.append(lax.dot_general((q * jnp.exp2(x)).astype(BF16), (k * jnp.exp2(-x)).astype(BF16), nt,
                              preferred_element_type=F32))
    st = st_ref[...]
    o_inter = lax.dot_general((q * jnp.exp2(b)).astype(BF16), st.astype(BF16), nt, preferred_element_type=F32)
    yield
    rows = []
    for a in range(GLA_C // GLA_LEAF):
        rs = slice(a * GLA_LEAF, (a + 1) * GLA_LEAF)
        ida = ids[rs]
        pa = jnp.where(ida == n_lv, sc[n_lv][rs], 0.0)
        for li, c in enumerate(levels):
            in_query_half = ((a * GLA_LEAF) // c) % 2 == (0 if backward else 1)
            if in_query_half:
                pa = jnp.where(ida == li, sc[li][rs], pa)
        rows.append(pa)
    p = jnp.concatenate(rows, axis=0)

    vb = v_ref[...]
    o_intra = jnp.dot(p.astype(BF16), vb, preferred_element_type=F32)
    k_end = (k * jnp.exp2(b_end - b)).astype(BF16)
    st_new = lax.dot_general(vb, k_end, tn, preferred_element_type=F32)
    yield
    o_ref[...] = o_inter + o_intra
    st_ref[...] = st * jnp.exp2(b_end) + st_new


def _interleave(units, skew, dep):
    pending = list(enumerate(units))
    active = []
    done = set()
    tick = 0
    while pending or active:
        if pending and tick % skew == 0:
            n = pending[0][0]
            assert n < dep or n - dep in done
            active.append(pending.pop(0))
        for n, g in list(active):
            try:
                next(g)
            except StopIteration:
                active.remove((n, g))
                done.add(n)
        tick += 1


def _gla_kernel(qf_ref, vf_ref, ff_ref, qb_ref, vb_ref, fb_ref, lbl_ref, ids_ref, s0_ref,
                of_ref, ob_ref, sfin_ref, st_s, b_s, *, layer, cps):
    j = pl.program_id(1)

    @pl.when(j == 0)
    def _():
        st_s[...] = s0_ref[...]

    row =lax.broadcasted_iota(jnp.int32, (GLA_C, GLA_C), 0)
    col = lax.broadcasted_iota(jnp.int32, (GLA_C, GLA_C), 1)
    units = [[[], []] for _ in range(cps)]
    for d, (q_ref, v_ref, f_ref, o_ref) in enumerate(((qf_ref, vf_ref, ff_ref, of_ref),
                                                      (qb_ref, vb_ref, fb_ref, ob_ref))):
        logits = lbl_ref[d]
        z = jnp.exp(logits - jnp.max(logits, axis=0, keepdims=True))
        sm = z / jnp.sum(z, axis=0, keepdims=True)
        lb = jnp.sum(sm[1:layer + 1], axis=0, keepdims=True) if layer > 0 else jnp.zeros((1, D), F32)
        backward = d == 1
        tri = jnp.where(col >= row, 1.0, 0.0) if backward else jnp.where(col <= row, 1.0, 0.0)
        tri = tri.astype(BF16)
        ids = ids_ref[d]
        for pos in range(cps):
            c = cps - 1 - pos if backward else pos
            rs = slice(c * GLA_C, (c + 1) * GLA_C)
            for h in range(H):
                hs = slice(h * DH, (h + 1) * DH)
                units[pos][d].append(_gla_head(q_ref.at[rs, hs], v_ref.at[rs, hs], f_ref.at[rs, hs],
                                               o_ref.at[rs, hs], lb[:, hs], tri, ids, st_s.at[d, h],
                                               b_s.at[pos, d, h], backward))
    _interleave([u for per_dir in units for pair in zip(*per_dir) for u in pair], skew=1, dep=2 * H)

    @pl.when(j == pl.num_programs(1) - 1)
    def _():
        sfin_ref[...] = st_s[...]


def _gla(proj_f, proj_b, lb_logits, s0, layer):
    bsz, length, _ = proj_f.shape
    cps = math.gcd(GLA_CPS, length // GLA_C)
    rows = cps * GLA_C
    nc = length // rows
    ids = jnp.asarray(_gla_level_ids())
    fwd = lambda col: pl.BlockSpec((None, rows, D), lambda b, j: (b, j, col))
    bwd = lambda col: pl.BlockSpec((None, rows, D), lambda b, j: (b, nc - 1 - j, col))
    return pl.pallas_call(
        functools.partial(_gla_kernel, layer=layer, cps=cps),
        grid=(bsz, nc),
        in_specs=[fwd(PQ), fwd(PI), fwd(0), bwd(PQ), bwd(PI), bwd(1),
                  pl.BlockSpec(lb_logits.shape, lambda b, j: (0, 0, 0)),
                  pl.BlockSpec((2, GLA_C, GLA_C), lambda b, j: (0, 0, 0)),
                  pl.BlockSpec((None, 2, H, DH, DH), lambda b, j: (b, 0, 0, 0, 0))],
        out_specs=[pl.BlockSpec((None, rows, D), lambda b, j: (b, j, 0)),
                   pl.BlockSpec((None, rows, D), lambda b, j: (b, nc - 1 - j, 0)),
                   pl.BlockSpec((None, 2, H, DH, DH), lambda b, j: (b, 0, 0, 0, 0))],
        out_shape=[jax.ShapeDtypeStruct((bsz, length, D), F32),
                   jax.ShapeDtypeStruct((bsz, length, D), F32),
                   jax.ShapeDtypeStruct((bsz, 2, H, DH, DH), F32)],
        scratch_shapes=[pltpu.VMEM((2, H, DH, DH), F32),
                        pltpu.VMEM((cps, 2, H, GLA_C, DH), F32)],
        compiler_params=_cparams(("arbitrary", "arbitrary")),
        name="gla",
    )(proj_b, proj_b, proj_f, proj_b, proj_b, proj_f, lb_logits, ids, s0)


def _mixer_kernel(*refs, tm, halo):
    if halo:
        (of_ref, ob_ref, g_ref, a_ref, b_ref, ap_ref, bp_ref, an_ref, bn_ref, gh_ref, gc_ref, x_ref, gt_ref,
         gnw_ref, dww_ref, dwb_ref, lnw_ref, lnb_ref, whg_ref, wcv_ref, wout_ref, o_ref, u_s, sh_s, u2_s) = refs
    else:
        (of_ref, ob_ref, g_ref, a_ref, b_ref, gh_ref, gc_ref, x_ref, gt_ref,
         gnw_ref, dww_ref, dwb_ref, lnw_ref, lnb_ref, whg_ref, wcv_ref, wout_ref, o_ref, u_s, sh_s, u2_s) = refs
    i = pl.program_id(1)
    n_tiles = pl.num_programs(1)

    o = of_ref[...] + ob_ref[...]
    parts = []
    for h in range(H):
        oh = o[:, h * DH:(h + 1) * DH]
        parts.append(oh * lax.rsqrt(jnp.mean(oh * oh, axis=-1, keepdims=True) + EPS))
    g = g_ref[...].astype(F32)
    oa =jnp.concatenate(parts, axis=-1) * gnw_ref[...] * (g * jax.nn.sigmoid(g))
    y_hg = jnp.dot(oa.astype(BF16), whg_ref[...], preferred_element_type=F32)

    glu = lambda a, b: a[...].astype(F32) * jax.nn.sigmoid(b[...].astype(F32))
    u_s[CONV_HALO:CONV_HALO + tm, :] = glu(a_ref, b_ref)
    zeros = jnp.zeros((CONV_HALO, D), F32)
    if halo:
        u_s[0:CONV_HALO, :] = jnp.where(i > 0, glu(ap_ref, bp_ref), zeros)
        u_s[CONV_HALO + tm:, :] = jnp.where(i < n_tiles - 1, glu(an_ref, bn_ref), zeros)
    else:
        u_s[0:CONV_HALO, :] = zeros
        u_s[CONV_HALO + tm:, :] = zeros
    ext = tm + 2 * CONV_HALO
    n_sh = ext - SUBLANES
    ub = u_s[...]
    for s in range(SUBLANES):
        rolled = ub if s == 0 else pltpu.roll(ub, ext - s, axis=0)
        sh_s[s] = rolled[:n_sh].reshape(n_sh // SUBLANES, SUBLANES, D)

    rb = CONV_RB // SUBLANES

    def conv_block(blk, carry):
        acc = jnp.broadcast_to(dwb_ref[...], (rb, SUBLANES, D))
        first = CONV_HALO - CONV_K // 2
        for s in range(SUBLANES):
            taps = [k for k in range(CONV_K) if (first + k) % SUBLANES == s]
            tiles = [(first + k) // SUBLANES for k in taps]
            win = sh_s[s, pl.ds(blk * rb + tiles[0], rb + tiles[-1] - tiles[0])]
            for k, t in zip(taps, tiles):
                acc = acc + dww_ref[k] * win[t - tiles[0]:t - tiles[0] + rb]
        mu = jnp.mean(acc, axis=-1, keepdims=True)
        xc = acc - mu
        ln = xc * lax.rsqrt(jnp.mean(xc * xc, axis=-1, keepdims=True) + EPS) * lnw_ref[...] + lnb_ref[...]
        u2 = (ln * jax.nn.sigmoid(ln)).reshape(CONV_RB, D)
        u2_s[pl.ds(pl.multiple_of(blk * CONV_RB, CONV_RB), CONV_RB), :] = u2.astype(BF16)
        return carry

    lax.fori_loop(0, tm // CONV_RB, conv_block, 0, unroll=True)
    y_cv = jnp.dot(u2_s[...], wcv_ref[...], preferred_element_type=F32)

    y = jax.nn.sigmoid(gh_ref[...].astype(F32)) * y_hg + jax.nn.sigmoid(gc_ref[...].astype(F32)) * y_cv
    o_ref[...] = x_ref[...] + gt_ref[...] * jnp.dot(y.astype(BF16), wout_ref[...], preferred_element_type=F32)


def _mixer(o_f, o_b, proj, x, gt, gnw, dww, dwb, lnw, lnb, whg, wcv, wout, tm):
    bsz, length, _ = x.shape
    n_tiles = length // tm
    halo = n_tiles > 1
    hb = tm // CONV_HALO
    n_hb = length // CONV_HALO
    row = lambda col: pl.BlockSpec((None, tm, D), lambda b, i: (b, i, col))
    prev = lambda col: pl.BlockSpec((None, CONV_HALO, D), lambda b, i: (b, jnp.maximum(i * hb - 1, 0), col))
    nxt = lambda col: pl.BlockSpec((None, CONV_HALO, D), lambda b, i: (b, jnp.minimum((i + 1) * hb, n_hb - 1), col))
    vec = lambda n: pl.BlockSpec((n, D), lambda b, i: (0, 0))
    mat = pl.BlockSpec((D, D), lambda b, i: (0, 0))
    in_specs = [row(0), row(0), row(PG), row(PA), row(PB)]
    args = [o_f, o_b, proj, proj, proj]
    if halo:
        in_specs += [prev(PA), prev(PB), nxt(PA), nxt(PB)]
        args += [proj, proj, proj, proj]
    in_specs += [row(PGH), row(PGC), row(0), pl.BlockSpec((None, 1, D), lambda b, i: (b, 0, 0)),
                 vec(1), pl.BlockSpec((CONV_K, SUBLANES, D), lambda b, i: (0, 0, 0)), vec(1), vec(1), vec(1),
                 mat, mat, mat]
    dww8 = jnp.broadcast_to(dww[:, None, :], (CONV_K, SUBLANES, D))
    args += [proj, proj, x, gt, gnw, dww8, dwb, lnw, lnb, whg, wcv, wout]
    ext = tm + 2 * CONV_HALO
    return pl.pallas_call(
        functools.partial(_mixer_kernel, tm=tm, halo=halo),
        grid=(bsz, n_tiles),
        in_specs=in_specs,
        out_specs=pl.BlockSpec((None, tm, D), lambda b, i: (b, i, 0)),
        out_shape=jax.ShapeDtypeStruct((bsz, length, D), F32),
        scratch_shapes=[pltpu.VMEM((ext, D), F32),
                        pltpu.VMEM((SUBLANES, (ext - SUBLANES) // SUBLANES, SUBLANES, D), F32),
                        pltpu.VMEM((tm, D), BF16)],
        compiler_params=_cparams(("arbitrary", "arbitrary")),
        name="mixer",
    )(*args)


def _ffn_kernel(*refs, tm, gw, halo, final):
    if halo:
        (x_ref, xp_ref, xn_ref, sc_ref, sh_ref, gt_ref, nw_ref, wup_ref, dww_ref, dwb_ref, wd_ref, fw_ref,
         o_ref, h_s, act_s) = refs
    else:
        (x_ref, sc_ref, sh_ref, gt_ref, nw_ref, wup_ref, dww_ref, dwb_ref, wd_ref, fw_ref,
         o_ref, h_s, act_s) = refs
    i = pl.program_id(1)
    n_tiles = pl.num_programs(1)
    norm = lambda x: _modulated_rmsnorm(x, nw_ref[...], sc_ref[...], sh_ref[...])

    h_s[gw:gw + tm, :] = norm(x_ref[...]).astype(BF16)
    zeros = jnp.zeros((gw, D), BF16)
    if halo:
        h_s[0:gw, :] = jnp.where(i > 0, norm(xp_ref[...]).astype(BF16), zeros)
        h_s[gw + tm:, :] = jnp.where(i < n_tiles - 1, norm(xn_ref[...]).astype(BF16), zeros)
    else:
        h_s[0:gw, :] = zeros
        h_s[gw + tm:, :] = zeros

    ext = tm + 2 * gw
    wpos = lax.broadcasted_iota(jnp.int32, (ext, FFN_TN), 0) % gw
    first_col = wpos == 0
    last_col = wpos == gw - 1

    nt = FFN // FFN_TN
    for n in range(nt):
        cs = slice(n * FFN_TN, (n + 1) * FFN_TN)
        vs = slice(FFN + n * FFN_TN, FFN + (n + 1) * FFN_TN)
        u = jnp.dot(h_s[...], wup_ref[:, cs], preferred_element_type=F32)
        v = jnp.dot(h_s[gw:gw + tm, :], wup_ref[:, vs], preferred_element_type=F32)
        u_l = jnp.where(first_col, 0.0, pltpu.roll(u, 1, axis=0))
        u_r = jnp.where(last_col, 0.0, pltpu.roll(u, ext - 1, axis=0))
        dw = dww_ref[:, cs]
        conv = jnp.broadcast_to(dwb_ref[:, cs], (tm, FFN_TN))
        for dy in range(3):
            lo = dy * gw
            conv = conv + dw[3 * dy:3 * dy + 1] * u_l[lo:lo + tm]
            conv = conv + dw[3 * dy + 1:3 * dy + 2] * u[lo:lo + tm]
            conv = conv + dw[3 * dy + 2:3 * dy + 3] * u_r[lo:lo + tm]
        act = 0.5 * conv * (1.0 + lax.erf(conv * np.float32(1.0 / np.sqrt(2.0)))) * v
        act_s[:, cs] = act.astype(BF16)

    down = jnp.dot(act_s[:, 0:FFN_TN], wd_ref[0:FFN_TN, :], preferred_element_type=F32)
    for n in range(1, nt):
        cs = slice(n * FFN_TN, (n + 1) * FFN_TN)
        down = down + jnp.dot(act_s[:, cs], wd_ref[cs, :], preferred_element_type=F32)
    y = x_ref[...] + gt_ref[...] * down
    if final:
        y = y * lax.rsqrt(jnp.mean(y * y, axis=-1, keepdims=True) + EPS) * fw_ref[...]
    o_ref[...] = y


def _conv_ffn(x, sc, sh, gt, nw, wup, dww, dwb, wd, fw, tm, gw, final):
    bsz, length, _ = x.shape
    n_tiles = length // tm
    halo = n_tiles > 1
    rb = tm // gw
    n_rows = length // gw
    const = lambda shape: pl.BlockSpec(shape, lambda b, i: (0,) * len(shape), pipeline_mode=pl.Buffered(1))
    per_b = pl.BlockSpec((None, 1, D), lambda b, i: (b, 0, 0))
    in_specs = [pl.BlockSpec((None, tm, D), lambda b, i: (b, i, 0))]
    args = [x]
    if halo:
        in_specs += [pl.BlockSpec((None, gw, D), lambda b, i: (b, jnp.maximum(i * rb - 1, 0), 0)),
                     pl.BlockSpec((None, gw, D), lambda b, i: (b, jnp.minimum((i + 1) * rb, n_rows - 1), 0))]
        args += [x, x]
    in_specs += [per_b, per_b, per_b, const((1, D)), const((D, 2 * FFN)), const((9, FFN)), const((1, FFN)),
                 const((FFN, D)), const((1, D))]
    args += [sc, sh, gt, nw, wup, dww, dwb, wd, fw]
    return pl.pallas_call(
        functools.partial(_ffn_kernel, tm=tm, gw=gw, halo=halo, final=final),
        grid=(bsz, n_tiles),
        in_specs=in_specs,
        out_specs=pl.BlockSpec((None, tm, D), lambda b, i: (b, i, 0)),
        out_shape=jax.ShapeDtypeStruct((bsz, length, D), F32),
        scratch_shapes=[pltpu.VMEM((tm + 2 * gw, D), BF16), pltpu.VMEM((tm, FFN), BF16)],
        compiler_params=_cparams(("arbitrary", "arbitrary")),
        name="conv_ffn",
    )(*args)


def kernel(x, c, ctx, c_ctx, w_mod, b_mod, norm1_w, w_in, hg_lb_logits, hg_gnorm_w, w_hg_out, cv_dw_w, cv_dw_b,
           cv_ln_w, cv_ln_b, w_cv_out, w_out, norm2_w, w_up, ffn_dw_w, ffn_dw_b, w_down, final_norm_w):
    bsz, length, _ = x.shape
    ctx_len = ctx.shape[1]
    depth = w_mod.shape[0]

    cc = jnp.zeros((8, D), F32).at[:bsz].set(c).at[bsz].set(c_ctx)
    mod = _modulation(cc, w_mod, b_mod)

    w_in_g = w_in.astype(BF16)
    whg_b, wcv_b, wout_b = w_hg_out.astype(BF16), w_cv_out.astype(BF16), w_out.astype(BF16)
    wup_b, wd_b = w_up.astype(BF16), w_down.astype(BF16)
    dww_t = ffn_dw_w.reshape(depth, 9, FFN)
    dwb_t = ffn_dw_b.reshape(depth, 1, FFN)
    gnw = jnp.tile(hg_gnorm_w, (1, H)).reshape(depth, 1, D)
    vec = lambda a, l: a[l].reshape(1, D)
    fw = final_norm_w.reshape(1, D)

    cx = ctx
    zero_state = jnp.zeros((bsz, 2, H, DH, DH), F32)
    for l in range(depth):
        m = mod[l].reshape(8, N_MOD, D)
        lat = lambda j: m[:bsz, j].reshape(bsz, 1, D)
        cxm = lambda j: jnp.broadcast_to(m[bsz, j].reshape(1, 1, D), (bsz, 1, D))
        last = l == depth - 1

        pcf, pcb = _in_projection(cx, cxm(1), cxm(0), vec(norm1_w, l), w_in_g[l], tm=ctx_len)
        pf, pb = _in_projection(x, lat(1), lat(0), vec(norm1_w, l), w_in_g[l], tm=512)
        ocf, ocb, s_ctx = _gla(pcf, pcb, hg_lb_logits, zero_state, l)
        o_f, o_b, _ = _gla(pf, pb, hg_lb_logits, s_ctx, l)
        mix_w = (gnw[l], cv_dw_w[l], vec(cv_dw_b, l), vec(cv_ln_w, l), vec(cv_ln_b, l), whg_b[l], wcv_b[l], wout_b[l])
        x = _mixer(o_f, o_b, pb, x, lat(2), *mix_w, tm=256)
        ffn_w = (vec(norm2_w, l), wup_b[l], dww_t[l], dwb_t[l], wd_b[l], fw)
        x = _conv_ffn(x, lat(4), lat(3), lat(5), *ffn_w, tm=512, gw=GRID_W, final=last)
        if not last:
            cx = _mixer(ocf, ocb, pcb, cx, cxm(2), *mix_w, tm=ctx_len)
            cx = _conv_ffn(cx, cxm(4), cxm(3), cxm(5), *ffn_w, tm=ctx_len, gw=ctx_len, final=False)
    return x
```

```python
import functools
import math

import numpy as np
import jax
import jax.numpy as jnp
from jax import lax
from jax.experimental import pallas as pl
from jax.experimental.pallas import tpu as pltpu

F32 = jnp.float32
BF16 = jnp.bfloat16

D = 1024
H = 8
DH = 128
N_MOD = 6
GRID_W = 64
CONV_K = 31
CONV_HALO = 16
CONV_RB = 32
SUBLANES = 8
FFN = 2816
FFN_TN = 256
IN_TM, MIX_TM, FFN_TM = 512, 256, 512
N_PROJ = 9
N_PROJ_F32 = 2
PROJ_SRC = (1, 2, 0, 3, 4, 5, 6, 7, 8)
PQ, PI, PG, PA, PB, PGH, PGC = range(7)
EPS = 1e-6
G_MIN = 1e-6

GLA_C = 128
GLA_CPS = 4
GLA_LEAF = 8
assert GLA_LEAF // 2 * -np.log2(G_MIN) < 120.0

VMEM_LIMIT = 56 * 1024 * 1024


def _cparams(sem):
    return pltpu.CompilerParams(dimension_semantics=sem, vmem_limit_bytes=VMEM_LIMIT)


def _mod_kernel(c_ref, w_ref, b_ref, o_ref):
    c = c_ref[...]
    a = (c * jax.nn.sigmoid(c)).astype(BF16)
    o_ref[...] = jnp.dot(a, w_ref[...].astype(BF16), preferred_element_type=F32) + b_ref[...]


def _modulation(cc, w_mod, b_mod):
    depth = w_mod.shape[0]
    tn = 1536
    return pl.pallas_call(
        _mod_kernel,
        grid=(depth, N_MOD * D // tn),
        in_specs=[pl.BlockSpec((8, D), lambda l, n: (0, 0)),
                  pl.BlockSpec((None, D, tn), lambda l, n: (l, 0, n)),
                  pl.BlockSpec((None, 1, tn), lambda l, n: (l, 0, n))],
        out_specs=pl.BlockSpec((None, 8, tn), lambda l, n: (l, 0, n)),
        out_shape=jax.ShapeDtypeStruct((depth, 8, N_MOD * D), F32),
        compiler_params=_cparams(("arbitrary", "arbitrary")),
        name="modulation",
    )(cc, w_mod, b_mod.reshape(depth, 1, N_MOD * D))


def _modulated_rmsnorm(x, nw, sc, sh):
    ms = jnp.mean(x * x, axis=-1, keepdims=True)
    return (x * lax.rsqrt(ms + EPS) * nw) * (1.0 + sc) + sh


def _inproj_kernel(x_ref, sc_ref, sh_ref, nw_ref, w_ref, of_ref, ob_ref):
    h = _modulated_rmsnorm(x_ref[...], nw_ref[...], sc_ref[...], sh_ref[...]).astype(BF16)
    for n, src in enumerate(PROJ_SRC):
        y = jnp.dot(h, w_ref[:, src * D:(src + 1) * D], preferred_element_type=F32)
        if n < N_PROJ_F32:
            of_ref[:, n * D:(n + 1) * D] = y
        else:
            m = n - N_PROJ_F32
            ob_ref[:, m * D:(m + 1) * D] = y.astype(BF16)


def _in_projection(x, sc, sh, nw, w_bf16, layer, tm):
    bsz, length, _ = x.shape
    n_b16 = N_PROJ - N_PROJ_F32
    return pl.pallas_call(
        _inproj_kernel,
        grid=(bsz, length // tm),
        in_specs=[pl.BlockSpec((None, tm, D), lambda b, i: (b, i, 0)),
                  pl.BlockSpec((None, 1, D), lambda b, i: (b, 0, 0)),
                  pl.BlockSpec((None, 1, D), lambda b, i: (b, 0, 0)),
                  pl.BlockSpec((1, D), lambda b, i: (0, 0)),
                  pl.BlockSpec((None, D, N_PROJ * D), lambda b, i: (layer, 0, 0), pipeline_mode=pl.Buffered(1))],
        out_specs=[pl.BlockSpec((None, tm, N_PROJ_F32 * D), lambda b, i: (b, i, 0)),
                   pl.BlockSpec((None, tm, n_b16 * D), lambda b, i: (b, i, 0))],
        out_shape=[jax.ShapeDtypeStruct((bsz, length, N_PROJ_F32 * D), F32),
                   jax.ShapeDtypeStruct((bsz, length, n_b16 * D), BF16)],
        compiler_params=_cparams(("arbitrary", "arbitrary")),
        name="in_projection",
    )(x, sc, sh, nw, w_bf16)


def _gla_levels():
    out = []
    c = GLA_C // 2
    while c >= GLA_LEAF:
        out.append(c)
        c //= 2
    return out


def _gla_level_ids():
    lv = _gla_levels()
    t = np.arange(GLA_C)[:, None]
    s = np.arange(GLA_C)[None, :]
    ids = np.full((GLA_C, GLA_C), -1, np.int32)
    for li, c in enumerate(lv):
        same = (t // (2 * c)) == (s // (2 * c))
        ids = np.where(same, li, ids)
    ids = np.where((t // GLA_LEAF) == (s // GLA_LEAF), len(lv), ids)
    ids = np.where(s <= t, ids, -1).astype(np.int32)
    return np.stack([ids, ids.T])


def _ref_rows(b_ref, c2, offset):
    parts = []
    for m in range(GLA_C // c2):
        p = m * c2 + offset
        parts.append(jnp.broadcast_to(b_ref[p:p + 1, :], (c2, DH)))
    return parts[0] if len(parts) == 1 else jnp.concatenate(parts, axis=0)


def _gla_head(q_ref, v_ref, f_ref, o_ref, lb, tri, ids, st_ref, b_ref, backward):
    levels = _gla_levels()
    n_lv = len(levels)
    f = f_ref[...]
    z = jnp.exp(-jnp.abs(f))
    s_big = 1.0 / (1.0 + z)
    s_small = z * s_big
    pos = f >= 0.0
    g = lb + (1.0 - lb) * jnp.where(pos, s_big, s_small)
    lg = jnp.log2(jnp.clip(g, G_MIN, 1.0))
    k = (1.0 - lb) * jnp.where(pos, s_small, s_big)
    hi = lg.astype(BF16)
    r1 = lg - hi.astype(F32)
    mid = r1.astype(BF16)
    lo = (r1 - mid.astype(F32)).astype(BF16)
    b = (jnp.dot(tri, hi, preferred_element_type=F32) + jnp.dot(tri, mid, preferred_element_type=F32)
         + jnp.dot(tri, lo, preferred_element_type=F32))
    yield
    b_ref[...] = b
    end_row = 0 if backward else GLA_C - 1
    b_end = b_ref[end_row:end_row + 1, :]
    q = q_ref[...].astype(F32)

    nt = (((1,), (1,)), ((), ()))
    tn = (((0,), (0,)), ((), ()))
    sc = []
    for li, c in enumerate(levels):
        parts = []
        for m in range(GLA_C // (2 * c)):
            p = 2 * c * m + (c if backward else c - 1)
            r = jnp.broadcast_to(b_ref[p:p + 1, :], (c, DH))
            first, second = b[2 * c * m:2 * c * m + c], b[2 * c * m + c:2 * c * (m + 1)]
            parts += [first - r, r - second] if backward else [r - first, second - r]
        e = jnp.exp2(jnp.concatenate(parts, axis=0))
        sc.append(lax.dot_general((q * e).astype(BF16), (k * e).astype(BF16), nt, preferred_element_type=F32))
        yield
    r = _ref_rows(b_ref, GLA_LEAF, GLA_LEAF // 2 if backward else GLA_LEAF // 2 - 1)
    x = b - r
    sc.append(lax.dot_general((q * jnp.exp2(x)).astype(BF16), (k * jnp.exp2(-x)).astype(BF16), nt,
                              preferred_element_type=F32))
    st = st_ref[...]
    o_inter = lax.dot_general((q * jnp.exp2(b)).astype(BF16), st.astype(BF16), nt, preferred_element_type=F32)
    yield
    rows = []
    for a in range(GLA_C // GLA_LEAF):
        rs = slice(a * GLA_LEAF, (a + 1) * GLA_LEAF)
        ida = ids[rs]
        pa = jnp.where(ida == n_lv, sc[n_lv][rs], 0.0)
        for li, c in enumerate(levels):
            in_query_half = ((a * GLA_LEAF) // c) % 2 == (0 if backward else 1)
            if in_query_half:
                pa = jnp.where(ida == li, sc[li][rs], pa)
        rows.append(pa)
    p = jnp.concatenate(rows, axis=0)

    vb = v_ref[...]
    o_intra = jnp.dot(p.astype(BF16), vb, preferred_element_type=F32)
    k_end = (k * jnp.exp2(b_end - b)).astype(BF16)
    st_new = lax.dot_general(vb, k_end, tn, preferred_element_type=F32)
    yield
    o_ref[...] = o_inter + o_intra
    st_ref[...] = st * jnp.exp2(b_end) + st_new


def _interleave(units, skew, dep):
    pending = list(enumerate(units))
    active = []
    done = set()
    tick = 0
    while pending or active:
        if pending and tick % skew == 0:
            n = pending[0][0]
            assert n < dep or n - dep in done
            active.append(pending.pop(0))
        for n, g in list(active):
            try:
                next(g)
            except StopIteration:
                active.remove((n, g))
                done.add(n)
        tick += 1


def _gla_kernel(qf_ref, vf_ref, ff_ref, qb_ref, vb_ref, fb_ref, lbl_ref, ids_ref, s0_ref,
                of_ref, ob_ref, sfin_ref, st_s, b_s, *, layer, cps):
    j = pl.program_id(1)

    @pl.when(j == 0)
    def _():
        st_s[...] = s0_ref[...]

    row = lax.broadcasted_iota(jnp.int32, (GLA_C, GLA_C), 0)
    col = lax.broadcasted_iota(jnp.int32, (GLA_C, GLA_C), 1)
    units = [[[], []] for _ in range(cps)]
    for d, (q_ref, v_ref, f_ref, o_ref) in enumerate(((qf_ref, vf_ref, ff_ref, of_ref),
                                                      (qb_ref, vb_ref, fb_ref, ob_ref))):
        logits = lbl_ref[d]
        z = jnp.exp(logits - jnp.max(logits, axis=0, keepdims=True))
        sm = z / jnp.sum(z, axis=0, keepdims=True)
        lb = jnp.sum(sm[1:layer + 1], axis=0, keepdims=True) if layer > 0 else jnp.zeros((1, D), F32)
        backward = d == 1
        tri = jnp.where(col >= row, 1.0, 0.0) if backward else jnp.where(col <= row, 1.0, 0.0)
        tri = tri.astype(BF16)
        ids = ids_ref[d]
        for pos in range(cps):
            c = cps - 1 - pos if backward else pos
            rs = slice(c * GLA_C, (c + 1) * GLA_C)
            for h in range(H):
                hs = slice(h * DH, (h + 1) * DH)
                units[pos][d].append(_gla_head(q_ref.at[rs, hs], v_ref.at[rs, hs], f_ref.at[rs, hs],
                                               o_ref.at[rs, hs], lb[:, hs], tri, ids, st_s.at[d, h],
                                               b_s.at[pos, d, h], backward))
    _interleave([u for per_dir in units for pair in zip(*per_dir) for u in pair], skew=1, dep=2 * H)

    @pl.when(j == pl.num_programs(1) - 1)
    def _():
        sfin_ref[...] = st_s[...]


def _gla(proj_f, proj_b, lb_logits, s0, layer):
    bsz, length, _ = proj_f.shape
    cps = math.gcd(GLA_CPS, length // GLA_C)
    rows = cps * GLA_C
    nc = length // rows
    ids = jnp.asarray(_gla_level_ids())
    fwd = lambda col: pl.BlockSpec((None, rows, D), lambda b, j: (b, j, col))
    bwd = lambda col: pl.BlockSpec((None, rows, D), lambda b, j: (b, nc - 1 - j, col))
    return pl.pallas_call(
        functools.partial(_gla_kernel, layer=layer, cps=cps),
        grid=(bsz, nc),
        in_specs=[fwd(PQ), fwd(PI), fwd(0), bwd(PQ), bwd(PI), bwd(1),
                  pl.BlockSpec(lb_logits.shape, lambda b, j: (0, 0, 0)),
                  pl.BlockSpec((2, GLA_C, GLA_C), lambda b, j: (0, 0, 0)),
                  pl.BlockSpec((None, 2, H, DH, DH), lambda b, j: (b, 0, 0, 0, 0))],
        out_specs=[pl.BlockSpec((None, rows, D), lambda b, j: (b, j, 0)),
                   pl.BlockSpec((None, rows, D), lambda b, j: (b, nc - 1 - j, 0)),
                   pl.BlockSpec((None, 2, H, DH, DH), lambda b, j: (b, 0, 0, 0, 0))],
        out_shape=[jax.ShapeDtypeStruct((bsz, length, D), F32),
                   jax.ShapeDtypeStruct((bsz, length, D), F32),
                   jax.ShapeDtypeStruct((bsz, 2, H, DH, DH), F32)],
        scratch_shapes=[pltpu.VMEM((2, H, DH, DH), F32),
                        pltpu.VMEM((cps, 2, H, GLA_C, DH), F32)],
        compiler_params=_cparams(("arbitrary", "arbitrary")),
        name="gla",
    )(proj_b, proj_b, proj_f, proj_b, proj_b, proj_f, lb_logits, ids, s0)


def _mixer_kernel(*refs, tm, halo):
    if halo:
        (of_ref, ob_ref, g_ref, a_ref, b_ref, ap_ref, bp_ref, an_ref, bn_ref, gh_ref, gc_ref, x_ref, gt_ref,
         gnw_ref, dww_ref, dwb_ref, lnw_ref, lnb_ref, whg_ref, wcv_ref, wout_ref, o_ref, u_s, sh_s, u2_s) = refs
    else:
        (of_ref, ob_ref, g_ref, a_ref, b_ref, gh_ref, gc_ref, x_ref, gt_ref,
         gnw_ref, dww_ref, dwb_ref, lnw_ref, lnb_ref, whg_ref, wcv_ref, wout_ref, o_ref, u_s, sh_s, u2_s) = refs
    i = pl.program_id(1)
    n_tiles = pl.num_programs(1)

    o = of_ref[...] + ob_ref[...]
    parts = []
    for h in range(H):
        oh = o[:, h * DH:(h + 1) * DH]
        parts.append(oh * lax.rsqrt(jnp.mean(oh * oh, axis=-1, keepdims=True) + EPS))
    g = g_ref[...].astype(F32)
    oa = jnp.concatenate(parts, axis=-1) * gnw_ref[...] * (g * jax.nn.sigmoid(g))
    y_hg = jnp.dot(oa.astype(BF16), whg_ref[...], preferred_element_type=F32)

    glu = lambda a, b: a[...].astype(F32) * jax.nn.sigmoid(b[...].astype(F32))
    u_s[CONV_HALO:CONV_HALO + tm, :] = glu(a_ref, b_ref)
    zeros = jnp.zeros((CONV_HALO, D), F32)
    if halo:
        u_s[0:CONV_HALO, :] = jnp.where(i > 0, glu(ap_ref, bp_ref), zeros)
        u_s[CONV_HALO + tm:, :] = jnp.where(i < n_tiles - 1, glu(an_ref, bn_ref), zeros)
    else:
        u_s[0:CONV_HALO, :] = zeros
        u_s[CONV_HALO + tm:, :] = zeros
    ext = tm + 2 * CONV_HALO
    n_sh = ext - SUBLANES
    ub = u_s[...]
    for s in range(SUBLANES):
        rolled = ub if s == 0 else pltpu.roll(ub, ext - s, axis=0)
        sh_s[s] = rolled[:n_sh].reshape(n_sh // SUBLANES, SUBLANES, D)

    rb = CONV_RB // SUBLANES

    def conv_block(blk, carry):
        acc = jnp.broadcast_to(dwb_ref[...], (rb, SUBLANES, D))
        first = CONV_HALO - CONV_K // 2
        for s in range(SUBLANES):
            taps = [k for k in range(CONV_K) if (first + k) % SUBLANES == s]
            tiles = [(first + k) // SUBLANES for k in taps]
            win = sh_s[s, pl.ds(blk * rb + tiles[0], rb + tiles[-1] - tiles[0])]
            for k, t in zip(taps, tiles):
                acc = acc + dww_ref[k] * win[t - tiles[0]:t - tiles[0] + rb]
        mu = jnp.mean(acc, axis=-1, keepdims=True)
        xc = acc - mu
        ln = xc * lax.rsqrt(jnp.mean(xc * xc, axis=-1, keepdims=True) + EPS) * lnw_ref[...] + lnb_ref[...]
        u2 = (ln * jax.nn.sigmoid(ln)).reshape(CONV_RB, D)
        u2_s[pl.ds(pl.multiple_of(blk * CONV_RB, CONV_RB), CONV_RB), :] = u2.astype(BF16)
        return carry

    lax.fori_loop(0, tm // CONV_RB, conv_block, 0, unroll=True)
    y_cv = jnp.dot(u2_s[...], wcv_ref[...], preferred_element_type=F32)

    y = jax.nn.sigmoid(gh_ref[...].astype(F32)) * y_hg + jax.nn.sigmoid(gc_ref[...].astype(F32)) * y_cv
    o_ref[...] = x_ref[...] + gt_ref[...] * jnp.dot(y.astype(BF16), wout_ref[...], preferred_element_type=F32)


def _mixer(o_f, o_b, proj, x, gt, gnw, dww, dwb, lnw, lnb, whg, wcv, wout, layer, tm):
    bsz, length, _ = x.shape
    n_tiles = length // tm
    halo = n_tiles > 1
    hb = tm // CONV_HALO
    n_hb = length // CONV_HALO
    row = lambda col: pl.BlockSpec((None, tm, D), lambda b, i: (b, i, col))
    prev = lambda col: pl.BlockSpec((None, CONV_HALO, D), lambda b, i: (b, jnp.maximum(i * hb - 1, 0), col))
    nxt = lambda col: pl.BlockSpec((None, CONV_HALO, D), lambda b, i: (b, jnp.minimum((i + 1) * hb, n_hb - 1), col))
    vec = lambda n: pl.BlockSpec((n, D), lambda b, i: (0, 0))
    mat = pl.BlockSpec((None, D, D), lambda b, i: (layer, 0, 0), pipeline_mode=pl.Buffered(1))
    in_specs = [row(0), row(0), row(PG), row(PA), row(PB)]
    args = [o_f, o_b, proj, proj, proj]
    if halo:
        in_specs += [prev(PA), prev(PB), nxt(PA), nxt(PB)]
        args += [proj, proj, proj, proj]
    in_specs += [row(PGH), row(PGC), row(0), pl.BlockSpec((None, 1, D), lambda b, i: (b, 0, 0)),
                 vec(1), pl.BlockSpec((CONV_K, SUBLANES, D), lambda b, i: (0, 0, 0)), vec(1), vec(1), vec(1),
                 mat, mat, mat]
    dww8 = jnp.broadcast_to(dww[:, None, :], (CONV_K, SUBLANES, D))
    args += [proj, proj, x, gt, gnw, dww8, dwb, lnw, lnb, whg, wcv, wout]
    ext = tm + 2 * CONV_HALO
    return pl.pallas_call(
        functools.partial(_mixer_kernel, tm=tm, halo=halo),
        grid=(bsz, n_tiles),
        in_specs=in_specs,
        out_specs=pl.BlockSpec((None, tm, D), lambda b, i: (b, i, 0)),
        out_shape=jax.ShapeDtypeStruct((bsz, length, D), F32),
        scratch_shapes=[pltpu.VMEM((ext, D), F32),
                        pltpu.VMEM((SUBLANES, (ext - SUBLANES) // SUBLANES, SUBLANES, D), F32),
                        pltpu.VMEM((tm, D), BF16)],
        compiler_params=_cparams(("arbitrary", "arbitrary")),
        name="mixer",
    )(*args)


def _ffn_kernel(*refs, tm, gw, halo, final):
    if halo:
        (x_ref, xp_ref, xn_ref, sc_ref, sh_ref, gt_ref, nw_ref, wup_ref, dww_ref, dwb_ref, wd_ref, fw_ref,
         o_ref, h_s, act_s) = refs
    else:
        (x_ref, sc_ref, sh_ref, gt_ref, nw_ref, wup_ref, dww_ref, dwb_ref, wd_ref, fw_ref,
         o_ref, h_s, act_s) = refs
    i = pl.program_id(1)
    n_tiles = pl.num_programs(1)
    norm = lambda x: _modulated_rmsnorm(x, nw_ref[...], sc_ref[...], sh_ref[...])

    h_s[gw:gw + tm, :] = norm(x_ref[...]).astype(BF16)
    zeros = jnp.zeros((gw, D), BF16)
    if halo:
        h_s[0:gw, :] = jnp.where(i > 0, norm(xp_ref[...]).astype(BF16), zeros)
        h_s[gw + tm:, :] = jnp.where(i < n_tiles - 1, norm(xn_ref[...]).astype(BF16), zeros)
    else:
        h_s[0:gw, :] = zeros
        h_s[gw + tm:, :] = zeros

    ext = tm + 2 * gw
    wpos = lax.broadcasted_iota(jnp.int32, (ext, FFN_TN), 0) % gw
    first_col = wpos == 0
    last_col = wpos == gw - 1

    nt = FFN // FFN_TN
    for n in range(nt):
        cs = slice(n * FFN_TN, (n + 1) * FFN_TN)
        vs = slice(FFN + n * FFN_TN, FFN + (n + 1) * FFN_TN)
        u = jnp.dot(h_s[...], wup_ref[:, cs], preferred_element_type=F32)
        v = jnp.dot(h_s[gw:gw + tm, :], wup_ref[:, vs], preferred_element_type=F32)
        u_l = jnp.where(first_col, 0.0, pltpu.roll(u, 1, axis=0))
        u_r = jnp.where(last_col, 0.0, pltpu.roll(u, ext - 1, axis=0))
        dw = dww_ref[:, cs]
        conv = jnp.broadcast_to(dwb_ref[:, cs], (tm, FFN_TN))
        for dy in range(3):
            lo = dy * gw
            conv = conv + dw[3 * dy:3 * dy + 1] * u_l[lo:lo + tm]
            conv = conv + dw[3 * dy + 1:3 * dy + 2] * u[lo:lo + tm]
            conv = conv + dw[3 * dy + 2:3 * dy + 3] * u_r[lo:lo + tm]
        act = 0.5 * conv * (1.0 + lax.erf(conv * np.float32(1.0 / np.sqrt(2.0)))) * v
        act_s[:, cs] = act.astype(BF16)

    down = jnp.dot(act_s[:, 0:FFN_TN], wd_ref[0:FFN_TN, :], preferred_element_type=F32)
    for n in range(1, nt):
        cs = slice(n * FFN_TN, (n + 1) * FFN_TN)
        down = down + jnp.dot(act_s[:, cs], wd_ref[cs, :], preferred_element_type=F32)
    y = x_ref[...] + gt_ref[...] * down
    if final:
        y = y * lax.rsqrt(jnp.mean(y * y, axis=-1, keepdims=True) + EPS) * fw_ref[...]
    o_ref[...] = y


def _conv_ffn(x, sc, sh, gt, nw, wup, dww, dwb, wd, fw, layer, tm, gw, final):
    bsz, length, _ = x.shape
    n_tiles = length // tm
    halo = n_tiles > 1
    rb = tm // gw
    n_rows = length // gw
    const = lambda shape: pl.BlockSpec(shape, lambda b, i: (0,) * len(shape), pipeline_mode=pl.Buffered(1))
    stacked = lambda shape: pl.BlockSpec((None,) + shape, lambda b, i: (layer,) + (0,) * len(shape),
                                         pipeline_mode=pl.Buffered(1))
    per_b = pl.BlockSpec((None, 1, D), lambda b, i: (b, 0, 0))
    in_specs = [pl.BlockSpec((None, tm, D), lambda b, i: (b, i, 0))]
    args = [x]
    if halo:
        in_specs += [pl.BlockSpec((None, gw, D), lambda b, i: (b, jnp.maximum(i * rb - 1, 0), 0)),
                     pl.BlockSpec((None, gw, D), lambda b, i: (b, jnp.minimum((i + 1) * rb, n_rows - 1), 0))]
        args += [x, x]
    in_specs += [per_b, per_b, per_b, const((1, D)), stacked((D, 2 * FFN)), stacked((9, FFN)), stacked((1, FFN)),
                 stacked((FFN, D)), const((1, D))]
    args += [sc, sh, gt, nw, wup, dww, dwb, wd, fw]
    return pl.pallas_call(
        functools.partial(_ffn_kernel, tm=tm, gw=gw, halo=halo, final=final),
        grid=(bsz, n_tiles),
        in_specs=in_specs,
        out_specs=pl.BlockSpec((None, tm, D), lambda b, i: (b, i, 0)),
        out_shape=jax.ShapeDtypeStruct((bsz, length, D), F32),
        scratch_shapes=[pltpu.VMEM((tm + 2 * gw, D), BF16), pltpu.VMEM((tm, FFN), BF16)],
        compiler_params=_cparams(("arbitrary", "arbitrary")),
        name="conv_ffn",
    )(*args)


def kernel(x, c, ctx, c_ctx, w_mod, b_mod, norm1_w, w_in, hg_lb_logits, hg_gnorm_w, w_hg_out, cv_dw_w, cv_dw_b,
           cv_ln_w, cv_ln_b, w_cv_out, w_out, norm2_w, w_up, ffn_dw_w, ffn_dw_b, w_down, final_norm_w):
    bsz, length, _ = x.shape
    ctx_len = ctx.shape[1]
    depth = w_mod.shape[0]

    cc = jnp.zeros((8, D), F32).at[:bsz].set(c).at[bsz].set(c_ctx)
    mod = _modulation(cc, w_mod, b_mod)

    w_in_g = w_in.astype(BF16)
    whg_b, wcv_b, wout_b = w_hg_out.astype(BF16), w_cv_out.astype(BF16), w_out.astype(BF16)
    wup_b, wd_b = w_up.astype(BF16), w_down.astype(BF16)
    dww_t = ffn_dw_w.reshape(depth, 9, FFN)
    dwb_t = ffn_dw_b.reshape(depth, 1, FFN)
    gnw = jnp.tile(hg_gnorm_w, (1, H)).reshape(depth, 1, D)
    vec = lambda a, l: a[l].reshape(1, D)
    fw = final_norm_w.reshape(1, D)

    cx = ctx
    zero_state = jnp.zeros((bsz, 2, H, DH, DH), F32)
    for l in range(depth):
        m = mod[l].reshape(8, N_MOD, D)
        lat = lambda j: m[:bsz, j].reshape(bsz, 1, D)
        cxm = lambda j: jnp.broadcast_to(m[bsz, j].reshape(1, 1, D), (bsz, 1, D))
        last = l == depth - 1

        pcf, pcb = _in_projection(cx, cxm(1), cxm(0), vec(norm1_w, l), w_in_g, l, tm=ctx_len)
        pf, pb = _in_projection(x, lat(1), lat(0), vec(norm1_w, l), w_in_g, l, tm=IN_TM)
        ocf, ocb, s_ctx = _gla(pcf, pcb, hg_lb_logits, zero_state, l)
        o_f, o_b, _ = _gla(pf, pb, hg_lb_logits, s_ctx, l)
        mix_w = (gnw[l], cv_dw_w[l], vec(cv_dw_b, l), vec(cv_ln_w, l), vec(cv_ln_b, l), whg_b, wcv_b, wout_b, l)
        x = _mixer(o_f, o_b, pb, x, lat(2), *mix_w, tm=MIX_TM)
        ffn_w = (vec(norm2_w, l), wup_b, dww_t, dwb_t, wd_b, fw, l)
        x = _conv_ffn(x, lat(4), lat(3), lat(5), *ffn_w, tm=FFN_TM, gw=GRID_W, final=last)
        if not last:
            cx = _mixer(ocf, ocb, pcb, cx, cxm(2), *mix_w, tm=ctx_len)
            cx = _conv_ffn(cx, cxm(4), cxm(3), cxm(5), *ffn_w, tm=ctx_len, gw=ctx_len, final=False)
    return x
```

```python
import functools
import math

import numpy as np
import jax
import jax.numpy as jnp
from jax import lax
from jax.experimental import pallas as pl
from jax.experimental.pallas import tpu as pltpu

F32 = jnp.float32
BF16 = jnp.bfloat16

D = 1024
H = 8
DH = 128
N_MOD = 6
GRID_W = 64
CONV_K = 31
CONV_HALO = 16
CONV_RB = 32
SUBLANES = 8
FFN = 2816
FFN_TN = 256
IN_TM, MIX_TM, FFN_TM = 512, 256, 512
N_PROJ = 9
N_PROJ_F32 = 2
PROJ_SRC = (1, 2, 0, 3, 4, 5, 6, 7, 8)
PQ, PI, PG, PA, PB, PGH, PGC = range(7)
EPS = 1e-6
G_MIN = 1e-6

GLA_C = 128
GLA_CPS = 4
GLA_LEAF = 8
assert GLA_LEAF // 2 * -np.log2(G_MIN) < 120.0

VMEM_LIMIT = 56 * 1024 * 1024


def _cparams(sem):
    return pltpu.CompilerParams(dimension_semantics=sem, vmem_limit_bytes=VMEM_LIMIT)


def _mod_kernel(c_ref, w_ref, b_ref, o_ref):
    c = c_ref[...]
    a = (c * jax.nn.sigmoid(c)).astype(BF16)
    o_ref[...] = jnp.dot(a, w_ref[...].astype(BF16), preferred_element_type=F32) + b_ref[...]


def _modulation(cc, w_mod, b_mod):
    depth = w_mod.shape[0]
    tn = 1536
    return pl.pallas_call(
        _mod_kernel,
        grid=(depth, N_MOD * D // tn),
        in_specs=[pl.BlockSpec((8, D), lambda l, n: (0, 0)),
                  pl.BlockSpec((None, D, tn), lambda l, n: (l, 0, n)),
                  pl.BlockSpec((None, 1, tn), lambda l, n: (l, 0, n))],
        out_specs=pl.BlockSpec((None, 8, tn), lambda l, n: (l, 0, n)),
        out_shape=jax.ShapeDtypeStruct((depth, 8, N_MOD * D), F32),
        compiler_params=_cparams(("arbitrary", "arbitrary")),
        name="modulation",
    )(cc, w_mod, b_mod.reshape(depth, 1, N_MOD * D))


def _modulated_rmsnorm(x, nw, sc, sh):
    ms = jnp.mean(x * x, axis=-1, keepdims=True)
    return (x * lax.rsqrt(ms + EPS) * nw) * (1.0 + sc) + sh


def _inproj_kernel(x_ref, sc_ref, sh_ref, nw_ref, w_ref, of_ref, ob_ref):
    h = _modulated_rmsnorm(x_ref[...], nw_ref[...], sc_ref[...], sh_ref[...]).astype(BF16)
    for n, src in enumerate(PROJ_SRC):
        y = jnp.dot(h, w_ref[:, src * D:(src + 1) * D], preferred_element_type=F32)
        if n < N_PROJ_F32:
            of_ref[:, n * D:(n + 1) * D] = y
        else:
            m = n - N_PROJ_F32
            ob_ref[:, m * D:(m + 1) * D] = y.astype(BF16)


def _in_projection(x, sc, sh, nw, w_bf16, layer, tm):
    bsz, length, _ = x.shape
    n_b16 = N_PROJ - N_PROJ_F32
    return pl.pallas_call(
        _inproj_kernel,
        grid=(bsz, length // tm),
        in_specs=[pl.BlockSpec((None, tm, D), lambda b, i: (b, i, 0)),
                  pl.BlockSpec((None, 1, D), lambda b, i: (b, 0, 0)),
                  pl.BlockSpec((None, 1, D), lambda b, i: (b, 0, 0)),
                  pl.BlockSpec((1, D), lambda b, i: (0, 0)),
                  pl.BlockSpec((None, D, N_PROJ * D), lambda b, i: (layer, 0, 0), pipeline_mode=pl.Buffered(1))],
        out_specs=[pl.BlockSpec((None, tm, N_PROJ_F32 * D), lambda b, i: (b, i, 0)),
                   pl.BlockSpec((None, tm, n_b16 * D), lambda b, i: (b, i, 0))],
        out_shape=[jax.ShapeDtypeStruct((bsz, length, N_PROJ_F32 * D), F32),
                   jax.ShapeDtypeStruct((bsz, length, n_b16 * D), BF16)],
        compiler_params=_cparams(("arbitrary", "arbitrary")),
        name="in_projection",
    )(x, sc, sh, nw, w_bf16)


def _gla_levels():
    out = []
    c = GLA_C // 2
    while c >= GLA_LEAF:
        out.append(c)
        c //= 2
    return out


def _gla_level_ids():
    lv = _gla_levels()
    t = np.arange(GLA_C)[:, None]
    s = np.arange(GLA_C)[None, :]
    ids = np.full((GLA_C, GLA_C), -1, np.int32)
    for li, c in enumerate(lv):
        same = (t // (2 * c)) == (s // (2 * c))
        ids = np.where(same, li, ids)
    ids = np.where((t // GLA_LEAF) == (s // GLA_LEAF), len(lv), ids)
    ids = np.where(s <= t, ids, -1).astype(np.int32)
    return np.stack([ids, ids.T])


def _ref_rows(b_ref, c2, offset):
    parts = []
    for m in range(GLA_C // c2):
        p = m * c2 + offset
        parts.append(jnp.broadcast_to(b_ref[p:p + 1, :], (c2, DH)))
    return parts[0] if len(parts) == 1 else jnp.concatenate(parts, axis=0)


def _gla_head(q_ref, v_ref, f_ref, o_ref, lb, tri, ids, st_ref, b_ref, backward):
    levels = _gla_levels()
    n_lv = len(levels)
    f = f_ref[...]
    z = jnp.exp(-jnp.abs(f))
    s_big = 1.0 / (1.0 + z)
    s_small = z * s_big
    pos = f >= 0.0
    g = lb + (1.0 - lb) * jnp.where(pos, s_big, s_small)
    lg = jnp.log2(jnp.clip(g, G_MIN, 1.0))
    k = (1.0 - lb) * jnp.where(pos, s_small, s_big)
    hi = lg.astype(BF16)
    r1 = lg - hi.astype(F32)
    mid = r1.astype(BF16)
    lo = (r1 - mid.astype(F32)).astype(BF16)
    b = (jnp.dot(tri, hi, preferred_element_type=F32) + jnp.dot(tri, mid, preferred_element_type=F32)
         + jnp.dot(tri, lo, preferred_element_type=F32))
    yield
    b_ref[...] = b
    end_row = 0 if backward else GLA_C - 1
    b_end = b_ref[end_row:end_row + 1, :]
    q = q_ref[...].astype(F32)

    nt = (((1,), (1,)), ((), ()))
    tn = (((0,), (0,)), ((), ()))
    n_blk = GLA_C // GLA_LEAF
    p_rows = [None] * n_blk

    def merge(li, s):
        for a in range(n_blk):
            c = levels[li] if li < n_lv else None
            if c is not None and ((a * GLA_LEAF) // c) % 2 != (0 if backward else 1):
                continue
            rs = slice(a * GLA_LEAF, (a + 1) * GLA_LEAF)
            p_rows[a] = jnp.where(ids[rs] == li, s[rs], 0.0 if p_rows[a] is None else p_rows[a])

    pending = None
    for li, c in enumerate(levels):
        parts = []
        for m in range(GLA_C // (2 * c)):
            p = 2 * c * m + (c if backward else c - 1)
            r = jnp.broadcast_to(b_ref[p:p + 1, :], (c, DH))
            first, second = b[2 * c * m:2 * c * m + c], b[2 * c * m + c:2 * c * (m + 1)]
            parts += [first - r, r - second] if backward else [r - first, second - r]
        e = jnp.exp2(jnp.concatenate(parts, axis=0))
        s = lax.dot_general((q * e).astype(BF16), (k * e).astype(BF16), nt, preferred_element_type=F32)
        if pending is not None:
            merge(*pending)
        pending = (li, s)
        yield
    r = _ref_rows(b_ref, GLA_LEAF, GLA_LEAF // 2 if backward else GLA_LEAF // 2 - 1)
    x = b - r
    s_leaf = lax.dot_general((q * jnp.exp2(x)).astype(BF16), (k * jnp.exp2(-x)).astype(BF16), nt,
                             preferred_element_type=F32)
    merge(*pending)
    st = st_ref[...]
    o_inter = lax.dot_general((q * jnp.exp2(b)).astype(BF16), st.astype(BF16), nt, preferred_element_type=F32)
    yield
    merge(n_lv, s_leaf)
    p = jnp.concatenate(p_rows, axis=0)

    vb = v_ref[...]
    o_intra = jnp.dot(p.astype(BF16), vb, preferred_element_type=F32)
    k_end = (k * jnp.exp2(b_end - b)).astype(BF16)
    st_new = lax.dot_general(vb, k_end, tn, preferred_element_type=F32)
    yield
    o_ref[...] = o_inter + o_intra
    st_ref[...] = st * jnp.exp2(b_end) + st_new


def _interleave(units, skew, dep):
    pending = list(enumerate(units))
    active = []
    done = set()
    tick = 0
    while pending or active:
        if pending and tick % skew == 0:
            n = pending[0][0]
            assert n < dep or n - dep in done
            active.append(pending.pop(0))
        for n, g in list(active):
            try:
                next(g)
            except StopIteration:
                active.remove((n, g))
                done.add(n)
        tick += 1


def _gla_kernel(qf_ref, vf_ref, ff_ref, qb_ref, vb_ref, fb_ref, lbl_ref, ids_ref, s0_ref,
                of_ref, ob_ref, sfin_ref, st_s, b_s, *, layer, cps):
    j = pl.program_id(1)

    @pl.when(j == 0)
    def _():
        st_s[...] = s0_ref[...]

    row = lax.broadcasted_iota(jnp.int32, (GLA_C, GLA_C), 0)
    col = lax.broadcasted_iota(jnp.int32, (GLA_C, GLA_C), 1)
    units = [[[], []] for _ in range(cps)]
    for d, (q_ref, v_ref, f_ref, o_ref) in enumerate(((qf_ref, vf_ref, ff_ref, of_ref),
                                                      (qb_ref, vb_ref, fb_ref, ob_ref))):
        logits = lbl_ref[d]
        z = jnp.exp(logits - jnp.max(logits, axis=0, keepdims=True))
        sm = z / jnp.sum(z, axis=0, keepdims=True)
        lb = jnp.sum(sm[1:layer + 1], axis=0, keepdims=True) if layer > 0 else jnp.zeros((1, D), F32)
        backward = d == 1
        tri = jnp.where(col >= row, 1.0, 0.0) if backward else jnp.where(col <= row, 1.0, 0.0)
        tri = tri.astype(BF16)
        ids = ids_ref[d]
        for pos in range(cps):
            c = cps - 1 - pos if backward else pos
            rs = slice(c * GLA_C, (c + 1) * GLA_C)
            for h in range(H):
                hs = slice(h * DH, (h + 1) * DH)
                units[pos][d].append(_gla_head(q_ref.at[rs, hs], v_ref.at[rs, hs], f_ref.at[rs, hs],
                                               o_ref.at[rs, hs], lb[:, hs], tri, ids, st_s.at[d, h],
                                               b_s.at[pos, d, h], backward))
    _interleave([u for per_dir in units for pair in zip(*per_dir) for u in pair], skew=1, dep=2 * H)

    @pl.when(j == pl.num_programs(1) - 1)
    def _():
        sfin_ref[...] = st_s[...]


def _gla(proj_f, proj_b, lb_logits, s0, layer):
    bsz, length, _ = proj_f.shape
    cps = math.gcd(GLA_CPS, length // GLA_C)
    rows = cps * GLA_C
    nc = length // rows
    ids = jnp.asarray(_gla_level_ids())
    fwd = lambda col: pl.BlockSpec((None, rows, D), lambda b, j: (b, j, col))
    bwd = lambda col: pl.BlockSpec((None, rows, D), lambda b, j: (b, nc - 1 - j, col))
    return pl.pallas_call(
        functools.partial(_gla_kernel, layer=layer, cps=cps),
        grid=(bsz, nc),
        in_specs=[fwd(PQ), fwd(PI), fwd(0), bwd(PQ), bwd(PI), bwd(1),
                  pl.BlockSpec(lb_logits.shape, lambda b, j: (0, 0, 0)),
                  pl.BlockSpec((2, GLA_C, GLA_C), lambda b, j: (0, 0, 0)),
                  pl.BlockSpec((None, 2, H, DH, DH), lambda b, j: (b, 0, 0, 0, 0))],
        out_specs=[pl.BlockSpec((None, rows, D), lambda b, j: (b, j, 0)),
                   pl.BlockSpec((None, rows, D), lambda b, j: (b, nc - 1 - j, 0)),
                   pl.BlockSpec((None, 2, H, DH, DH), lambda b, j: (b, 0, 0, 0, 0))],
        out_shape=[jax.ShapeDtypeStruct((bsz, length, D), F32),
                   jax.ShapeDtypeStruct((bsz, length, D), F32),
                   jax.ShapeDtypeStruct((bsz, 2, H, DH, DH), F32)],
        scratch_shapes=[pltpu.VMEM((2, H, DH, DH), F32),
                        pltpu.VMEM((cps, 2, H, GLA_C, DH), F32)],
        compiler_params=_cparams(("arbitrary", "arbitrary")),
        name="gla",
    )(proj_b, proj_b, proj_f, proj_b, proj_b, proj_f, lb_logits, ids, s0)


def _mixer_kernel(*refs, tm, halo):
    if halo:
        (of_ref, ob_ref, g_ref, a_ref, b_ref, ap_ref, bp_ref, an_ref, bn_ref, gh_ref, gc_ref, x_ref, gt_ref,
         gnw_ref, dww_ref, dwb_ref, lnw_ref, lnb_ref, whg_ref, wcv_ref, wout_ref, o_ref, u_s, sh_s, u2_s) = refs
    else:
        (of_ref, ob_ref, g_ref, a_ref, b_ref, gh_ref, gc_ref, x_ref, gt_ref,
         gnw_ref, dww_ref, dwb_ref, lnw_ref, lnb_ref, whg_ref, wcv_ref, wout_ref, o_ref, u_s, sh_s, u2_s) = refs
    i = pl.program_id(1)
    n_tiles = pl.num_programs(1)

    o = of_ref[...] + ob_ref[...]
    parts = []
    for h in range(H):
        oh = o[:, h * DH:(h + 1) * DH]
        parts.append(oh * lax.rsqrt(jnp.mean(oh * oh, axis=-1, keepdims=True) + EPS))
    g = g_ref[...].astype(F32)
    oa = jnp.concatenate(parts, axis=-1) * gnw_ref[...] * (g * jax.nn.sigmoid(g))
    y_hg = jnp.dot(oa.astype(BF16), whg_ref[...], preferred_element_type=F32)

    glu = lambda a, b: a[...].astype(F32) * jax.nn.sigmoid(b[...].astype(F32))
    u_s[CONV_HALO:CONV_HALO + tm, :] = glu(a_ref, b_ref)
    zeros = jnp.zeros((CONV_HALO, D), F32)
    if halo:
        u_s[0:CONV_HALO, :] = jnp.where(i > 0, glu(ap_ref, bp_ref), zeros)
        u_s[CONV_HALO + tm:, :] = jnp.where(i < n_tiles - 1, glu(an_ref, bn_ref), zeros)
    else:
        u_s[0:CONV_HALO, :] = zeros
        u_s[CONV_HALO + tm:, :] = zeros
    ext = tm + 2 * CONV_HALO
    n_sh = ext - SUBLANES
    ub = u_s[...]
    for s in range(SUBLANES):
        rolled = ub if s == 0 else pltpu.roll(ub, ext - s, axis=0)
        sh_s[s] = rolled[:n_sh].reshape(n_sh // SUBLANES, SUBLANES, D)

    rb = CONV_RB // SUBLANES

    def conv_block(blk, carry):
        acc = jnp.broadcast_to(dwb_ref[...], (rb, SUBLANES, D))
        first = CONV_HALO - CONV_K // 2
        for s in range(SUBLANES):
            taps = [k for k in range(CONV_K) if (first + k) % SUBLANES == s]
            tiles = [(first + k) // SUBLANES for k in taps]
            win = sh_s[s, pl.ds(blk * rb + tiles[0], rb + tiles[-1] - tiles[0])]
            for k, t in zip(taps, tiles):
                acc = acc + dww_ref[k] * win[t - tiles[0]:t - tiles[0] + rb]
        mu = jnp.mean(acc, axis=-1, keepdims=True)
        xc = acc - mu
        ln = xc * lax.rsqrt(jnp.mean(xc * xc, axis=-1, keepdims=True) + EPS) * lnw_ref[...] + lnb_ref[...]
        u2 = (ln * jax.nn.sigmoid(ln)).reshape(CONV_RB, D)
        u2_s[pl.ds(pl.multiple_of(blk * CONV_RB, CONV_RB), CONV_RB), :] = u2.astype(BF16)
        return carry

    lax.fori_loop(0, tm // CONV_RB, conv_block, 0, unroll=True)
    y_cv = jnp.dot(u2_s[...], wcv_ref[...], preferred_element_type=F32)

    y = jax.nn.sigmoid(gh_ref[...].astype(F32)) * y_hg + jax.nn.sigmoid(gc_ref[...].astype(F32)) * y_cv
    o_ref[...] = x_ref[...] + gt_ref[...] * jnp.dot(y.astype(BF16), wout_ref[...], preferred_element_type=F32)


def _mixer(o_f, o_b, proj, x, gt, gnw, dww, dwb, lnw, lnb, whg, wcv, wout, layer, tm):
    bsz, length, _ = x.shape
    n_tiles = length // tm
    halo = n_tiles > 1
    hb = tm // CONV_HALO
    n_hb = length // CONV_HALO
    row = lambda col: pl.BlockSpec((None, tm, D), lambda b, i: (b, i, col))
    prev = lambda col: pl.BlockSpec((None, CONV_HALO, D), lambda b, i: (b, jnp.maximum(i * hb - 1, 0), col))
    nxt = lambda col: pl.BlockSpec((None, CONV_HALO, D), lambda b, i: (b, jnp.minimum((i + 1) * hb, n_hb - 1), col))
    vec = lambda n: pl.BlockSpec((n, D), lambda b, i: (0, 0))
    mat = pl.BlockSpec((None, D, D), lambda b, i: (layer, 0, 0), pipeline_mode=pl.Buffered(1))
    in_specs = [row(0), row(0), row(PG), row(PA), row(PB)]
    args = [o_f, o_b, proj, proj, proj]
    if halo:
        in_specs += [prev(PA), prev(PB), nxt(PA), nxt(PB)]
        args += [proj, proj, proj, proj]
    in_specs += [row(PGH), row(PGC), row(0), pl.BlockSpec((None, 1, D), lambda b, i: (b, 0, 0)),
                 vec(1), pl.BlockSpec((CONV_K, SUBLANES, D), lambda b, i: (0, 0, 0)), vec(1), vec(1), vec(1),
                 mat, mat, mat]
    dww8 = jnp.broadcast_to(dww[:, None, :], (CONV_K, SUBLANES, D))
    args += [proj, proj, x, gt, gnw, dww8, dwb, lnw, lnb, whg, wcv, wout]
    ext = tm + 2 * CONV_HALO
    return pl.pallas_call(
        functools.partial(_mixer_kernel, tm=tm, halo=halo),
        grid=(bsz, n_tiles),
        in_specs=in_specs,
        out_specs=pl.BlockSpec((None, tm, D), lambda b, i: (b, i, 0)),
        out_shape=jax.ShapeDtypeStruct((bsz, length, D), F32),
        scratch_shapes=[pltpu.VMEM((ext, D), F32),
                        pltpu.VMEM((SUBLANES, (ext - SUBLANES) // SUBLANES, SUBLANES, D), F32),
                        pltpu.VMEM((tm, D), BF16)],
        compiler_params=_cparams(("arbitrary", "arbitrary")),
        name="mixer",
    )(*args)


def _ffn_kernel(*refs, tm, gw, halo, final):
    if halo:
        (x_ref, xp_ref, xn_ref, sc_ref, sh_ref, gt_ref, nw_ref, wup_ref, dww_ref, dwb_ref, wd_ref, fw_ref,
         o_ref, h_s, act_s) = refs
    else:
        (x_ref, sc_ref, sh_ref, gt_ref, nw_ref, wup_ref, dww_ref, dwb_ref, wd_ref, fw_ref,
         o_ref, h_s, act_s) = refs
    i = pl.program_id(1)
    n_tiles = pl.num_programs(1)
    norm = lambda x: _modulated_rmsnorm(x, nw_ref[...], sc_ref[...], sh_ref[...])

    h_s[gw:gw + tm, :] = norm(x_ref[...]).astype(BF16)
    zeros = jnp.zeros((gw, D), BF16)
    if halo:
        h_s[0:gw, :] = jnp.where(i > 0, norm(xp_ref[...]).astype(BF16), zeros)
        h_s[gw + tm:, :] = jnp.where(i < n_tiles - 1, norm(xn_ref[...]).astype(BF16), zeros)
    else:
        h_s[0:gw, :] = zeros
        h_s[gw + tm:, :] = zeros

    ext = tm + 2 * gw
    wpos = lax.broadcasted_iota(jnp.int32, (ext, FFN_TN), 0) % gw
    first_col = wpos == 0
    last_col = wpos == gw - 1

    nt = FFN // FFN_TN
    for n in range(nt):
        cs = slice(n * FFN_TN, (n + 1) * FFN_TN)
        vs = slice(FFN + n * FFN_TN, FFN + (n + 1) * FFN_TN)
        u = jnp.dot(h_s[...], wup_ref[:, cs], preferred_element_type=F32)
        v = jnp.dot(h_s[gw:gw + tm, :], wup_ref[:, vs], preferred_element_type=F32)
        u_l = jnp.where(first_col, 0.0, pltpu.roll(u, 1, axis=0))
        u_r = jnp.where(last_col, 0.0, pltpu.roll(u, ext - 1, axis=0))
        dw = dww_ref[:, cs]
        conv = jnp.broadcast_to(dwb_ref[:, cs], (tm, FFN_TN))
        for dy in range(3):
            lo = dy * gw
            conv = conv + dw[3 * dy:3 * dy + 1] * u_l[lo:lo + tm]
            conv = conv + dw[3 * dy + 1:3 * dy + 2] * u[lo:lo + tm]
            conv = conv + dw[3 * dy + 2:3 * dy + 3] * u_r[lo:lo + tm]
        act = 0.5 * conv * (1.0 + lax.erf(conv * np.float32(1.0 / np.sqrt(2.0)))) * v
        act_s[:, cs] = act.astype(BF16)

    down = jnp.dot(act_s[:, 0:FFN_TN], wd_ref[0:FFN_TN, :], preferred_element_type=F32)
    for n in range(1, nt):
        cs = slice(n * FFN_TN, (n + 1) * FFN_TN)
        down = down + jnp.dot(act_s[:, cs], wd_ref[cs, :], preferred_element_type=F32)
    y = x_ref[...] + gt_ref[...] * down
    if final:
        y = y * lax.rsqrt(jnp.mean(y * y, axis=-1, keepdims=True) + EPS) * fw_ref[...]
    o_ref[...] = y


def _conv_ffn(x, sc, sh, gt, nw, wup, dww, dwb, wd, fw, layer, tm, gw, final):
    bsz, length, _ = x.shape
    n_tiles = length // tm
    halo = n_tiles > 1
    rb = tm // gw
    n_rows = length // gw
    const = lambda shape: pl.BlockSpec(shape, lambda b, i: (0,) * len(shape), pipeline_mode=pl.Buffered(1))
    stacked = lambda shape: pl.BlockSpec((None,) + shape, lambda b, i: (layer,) + (0,) * len(shape),
                                         pipeline_mode=pl.Buffered(1))
    per_b = pl.BlockSpec((None, 1, D), lambda b, i: (b, 0, 0))
    in_specs = [pl.BlockSpec((None, tm, D), lambda b, i: (b, i, 0))]
    args = [x]
    if halo:
        in_specs += [pl.BlockSpec((None, gw, D), lambda b, i: (b, jnp.maximum(i * rb - 1, 0), 0)),
                     pl.BlockSpec((None, gw, D), lambda b, i: (b, jnp.minimum((i + 1) * rb, n_rows - 1), 0))]
        args += [x, x]
    in_specs += [per_b, per_b, per_b, const((1, D)), stacked((D, 2 * FFN)), stacked((9, FFN)), stacked((1, FFN)),
                 stacked((FFN, D)), const((1, D))]
    args += [sc, sh, gt, nw, wup, dww, dwb, wd, fw]
    return pl.pallas_call(
        functools.partial(_ffn_kernel, tm=tm, gw=gw, halo=halo, final=final),
        grid=(bsz, n_tiles),
        in_specs=in_specs,
        out_specs=pl.BlockSpec((None, tm, D), lambda b, i: (b, i, 0)),
        out_shape=jax.ShapeDtypeStruct((bsz, length, D), F32),
        scratch_shapes=[pltpu.VMEM((tm + 2 * gw, D), BF16), pltpu.VMEM((tm, FFN), BF16)],
        compiler_params=_cparams(("arbitrary", "arbitrary")),
        name="conv_ffn",
    )(*args)


def kernel(x, c, ctx, c_ctx, w_mod, b_mod, norm1_w, w_in, hg_lb_logits, hg_gnorm_w, w_hg_out, cv_dw_w, cv_dw_b,
           cv_ln_w, cv_ln_b, w_cv_out, w_out, norm2_w, w_up, ffn_dw_w, ffn_dw_b, w_down, final_norm_w):
    bsz, length, _ = x.shape
    ctx_len = ctx.shape[1]
    depth = w_mod.shape[0]

    cc = jnp.zeros((8, D), F32).at[:bsz].set(c).at[bsz].set(c_ctx)
    mod = _modulation(cc, w_mod, b_mod)

    w_in_g = w_in.astype(BF16)
    whg_b, wcv_b, wout_b = w_hg_out.astype(BF16), w_cv_out.astype(BF16), w_out.astype(BF16)
    wup_b, wd_b = w_up.astype(BF16), w_down.astype(BF16)
    dww_t = ffn_dw_w.reshape(depth, 9, FFN)
    dwb_t = ffn_dw_b.reshape(depth, 1, FFN)
    gnw = jnp.tile(hg_gnorm_w, (1, H)).reshape(depth, 1, D)
    vec = lambda a, l: a[l].reshape(1, D)
    fw = final_norm_w.reshape(1, D)

    cx = ctx
    zero_state = jnp.zeros((bsz, 2, H, DH, DH), F32)
    for l in range(depth):
        m = mod[l].reshape(8, N_MOD, D)
        lat = lambda j: m[:bsz, j].reshape(bsz, 1, D)
        cxm = lambda j: jnp.broadcast_to(m[bsz, j].reshape(1, 1, D), (bsz, 1, D))
        last = l == depth - 1

        pcf, pcb = _in_projection(cx, cxm(1), cxm(0), vec(norm1_w, l), w_in_g, l, tm=ctx_len)
        pf, pb = _in_projection(x, lat(1), lat(0), vec(norm1_w, l), w_in_g, l, tm=IN_TM)
        ocf, ocb, s_ctx = _gla(pcf, pcb, hg_lb_logits, zero_state, l)
        o_f, o_b, _ = _gla(pf, pb, hg_lb_logits, s_ctx, l)
        mix_w = (gnw[l], cv_dw_w[l], vec(cv_dw_b, l), vec(cv_ln_w, l), vec(cv_ln_b, l), whg_b, wcv_b, wout_b, l)
        x = _mixer(o_f, o_b, pb, x, lat(2), *mix_w, tm=MIX_TM)
        ffn_w = (vec(norm2_w, l), wup_b, dww_t, dwb_t, wd_b, fw, l)
        x = _conv_ffn(x, lat(4), lat(3), lat(5), *ffn_w, tm=FFN_TM, gw=GRID_W, final=last)
        if not last:
            cx = _mixer(ocf, ocb, pcb, cx, cxm(2), *mix_w, tm=ctx_len)
            cx = _conv_ffn(cx, cxm(4), cxm(3), cxm(5), *ffn_w, tm=ctx_len, gw=ctx_len, final=False)
    return x
```
